```python
import math
import jax, jax.numpy as jnp
from jax import lax
import numpy as np

D_MODEL = 1024
BATCH = 8
SEQ = 2048
DEPTH = 1

SSM_HEADS = 8
SSM_HEAD_DIM = 64
D_SSM = SSM_HEADS * SSM_HEAD_DIM
SSM_GROUPS = 2
D_STATE = 128
CONV_WIDTH = 4
CHUNK = 128
D_CONV = D_SSM + 2 * SSM_GROUPS * D_STATE
ATTN_HEADS = 8
KV_HEADS = 2
HEAD_DIM = 64
D_ATTN = ATTN_HEADS * HEAD_DIM
D_KV = KV_HEADS * HEAD_DIM
WINDOW = 128
D_MIX = D_SSM + D_ATTN
D_IN_PROJ = D_SSM + D_CONV + SSM_HEADS + D_ATTN + 2 * D_KV
D_FF = -(-8 * D_MODEL // (3 * 256)) * 256
EPS = 1e-5

kernel_name = "hymba_ssd_swa_sink_hybrid"


def rms_norm(x, w):
    xf = x.astype(jnp.float32)
    y = xf * lax.rsqrt(jnp.mean(xf * xf, axis=-1, keepdims=True) + EPS)
    return (y * w.astype(jnp.float32)).astype(x.dtype)


def causal_depthwise_conv(u, w, b):
    k_width, ch = w.shape
    y = lax.conv_general_dilated(
        u, w[:, None, :].astype(u.dtype), window_strides=(1,),
        padding=[(k_width - 1, 0)], dimension_numbers=("NWC", "WIO", "NWC"),
        feature_group_count=ch)
    return y + b.astype(u.dtype)


def ssd_chunked(x, dt, a, b_mat, c_mat):
    bsz, seq, n_h, p = x.shape
    n_g, n_s = b_mat.shape[2], b_mat.shape[3]
    r = n_h // n_g
    nc = seq // CHUNK
    xdt = (x.astype(jnp.float32) * dt[..., None]).reshape(bsz, nc, CHUNK, n_g, r, p)
    bc = b_mat.astype(jnp.float32).reshape(bsz, nc, CHUNK, n_g, n_s)
    cc = c_mat.astype(jnp.float32).reshape(bsz, nc, CHUNK, n_g, n_s)
    d_a = (dt * a).reshape(bsz, nc, CHUNK, n_g, r).transpose(0, 3, 4, 1, 2)
    a_cum = jnp.cumsum(d_a, axis=-1)
    causal = jnp.tril(jnp.ones((CHUNK, CHUNK), dtype=bool))
    seg = a_cum[..., :, None] - a_cum[..., None, :]
    decay_in = jnp.exp(jnp.where(causal, seg, -jnp.inf))
    cb = jnp.einsum("bctgn,bcsgn->bgcts", cc, bc)
    y_diag = jnp.einsum("bgcts,bgrcts,bcsgrp->bctgrp", cb, decay_in, xdt)
    decay_to_end = jnp.exp(a_cum[..., -1:] - a_cum)
    chunk_states = jnp.einsum("bctgn,bgrct,bctgrp->bcgrpn", bc, decay_to_end, xdt)
    chunk_decay = jnp.exp(a_cum[..., -1])

    def step(h, inp):
        s_c, d_c = inp
        return h * d_c[..., None, None] + s_c, h

    h0 = jnp.zeros((bsz, n_g, r, p, n_s), jnp.float32)
    _, prev_states = lax.scan(step, h0, (jnp.moveaxis(chunk_states, 1, 0),
                                         jnp.moveaxis(chunk_decay, 3, 0)))
    y_off = jnp.einsum("bctgn,cbgrpn,bgrct->bctgrp", cc, prev_states, jnp.exp(a_cum))
    return (y_diag + y_off).reshape(bsz, seq, n_h, p)


def sliding_window_attention_with_sinks(q, k, v, sinks):
    bsz, seq, n_q, d = q.shape
    n_kv = k.shape[2]
    r = n_q // n_kv
    nb = seq // WINDOW
    qb = q.reshape(bsz, nb, WINDOW, n_kv, r, d)

    def with_prev(t):
        tb = t.reshape(bsz, nb, WINDOW, n_kv, d)
        prev = jnp.pad(tb[:, :-1], ((0, 0), (1, 0), (0, 0), (0, 0), (0, 0)))
        return jnp.concatenate([prev, tb], axis=2)

    kb, vb = with_prev(k), with_prev(v)
    scores = jnp.einsum("bnqhrd,bnkhd->bnhrqk", qb, kb).astype(jnp.float32) * (d ** -0.5)
    q_pos = jnp.arange(WINDOW)[:, None] + WINDOW
    k_pos = jnp.arange(2 * WINDOW)[None, :]
    band = (k_pos <= q_pos) & (k_pos > q_pos - WINDOW)
    k_global = (jnp.arange(nb) * WINDOW - WINDOW)[:, None] + k_pos
    mask = band[None] & (k_global >= 0)[:, None, :]
    scores = jnp.where(mask[None, :, None, None], scores, -jnp.inf)
    sink = sinks.astype(jnp.float32).reshape(n_kv, r)[None, None, :, :, None, None]
    m = jnp.maximum(jnp.max(scores, axis=-1, keepdims=True), sink)
    p = jnp.exp(scores - m)
    probs = p / (jnp.sum(p, axis=-1, keepdims=True) + jnp.exp(sink - m))
    out = jnp.einsum("bnhrqk,bnkhd->bnqhrd", probs.astype(v.dtype), vb)
    return out.reshape(bsz, seq, n_q * d)


def hybrid_layer(x, norm_mix_w, w_in, conv_w, conv_b, dt_bias, a_log, d_skip,
                 ssm_norm_w, attn_sinks, w_out, norm_ffn_w, w_gate, w_up, w_down):
    bsz, seq, _ = x.shape
    h = rms_norm(x, norm_mix_w)
    proj = h @ w_in
    s0 = D_SSM
    s1 = s0 + D_CONV
    s2 = s1 + SSM_HEADS
    s3 = s2 + D_ATTN
    s4 = s3 + D_KV
    z, xbc, dt_raw, q, k, v = jnp.split(proj, [s0, s1, s2, s3, s4], axis=-1)

    xbc = jax.nn.silu(causal_depthwise_conv(xbc, conv_w, conv_b))
    xs, bm, cm = jnp.split(xbc, [D_SSM, D_SSM + SSM_GROUPS * D_STATE], axis=-1)
    xs = xs.reshape(bsz, seq, SSM_HEADS, SSM_HEAD_DIM)
    bm = bm.reshape(bsz, seq, SSM_GROUPS, D_STATE)
    cm = cm.reshape(bsz, seq, SSM_GROUPS, D_STATE)
    dt = jax.nn.softplus(dt_raw.astype(jnp.float32) + dt_bias.astype(jnp.float32))
    a = -jnp.exp(a_log.astype(jnp.float32))
    y = ssd_chunked(xs, dt, a, bm, cm) + d_skip.astype(jnp.float32)[:, None] * xs.astype(jnp.float32)
    y = y.reshape(bsz, seq, D_SSM) * jax.nn.silu(z.astype(jnp.float32))
    y_ssm = rms_norm(y.reshape(bsz, seq, SSM_GROUPS, D_SSM // SSM_GROUPS),
                     ssm_norm_w.reshape(SSM_GROUPS, D_SSM // SSM_GROUPS)).reshape(bsz, seq, D_SSM)

    y_attn = sliding_window_attention_with_sinks(
        q.reshape(bsz, seq, ATTN_HEADS, HEAD_DIM),
        k.reshape(bsz, seq, KV_HEADS, HEAD_DIM),
        v.reshape(bsz, seq, KV_HEADS, HEAD_DIM), attn_sinks)

    mixed = jnp.concatenate([y_ssm.astype(x.dtype), y_attn.astype(x.dtype)], axis=-1) @ w_out
    x = x + mixed

    h = rms_norm(x, norm_ffn_w)
    x = x + (jax.nn.silu(h @ w_gate) * (h @ w_up)) @ w_down
    return x


def setup_inputs(seed: int = 0) -> dict:
    key = jax.random.key(seed)
    ks = jax.random.split(key, 20)
    f32 = jnp.float32

    def normal(k, shape, scale):
        return jax.random.normal(k, shape, f32) * scale

    x = jax.random.normal(ks[0], (BATCH, SEQ, D_MODEL), f32)
    norm_mix_w = 1.0 + normal(ks[1], (DEPTH, D_MODEL), 0.02)
    w_in = normal(ks[2], (DEPTH, D_MODEL, D_IN_PROJ), D_MODEL ** -0.5)
    conv_w = normal(ks[3], (DEPTH, CONV_WIDTH, D_CONV), CONV_WIDTH ** -0.5)
    conv_b = normal(ks[4], (DEPTH, D_CONV), 0.02)
    u = jax.random.uniform(ks[5], (DEPTH, SSM_HEADS), f32)
    dt0 = jnp.exp(u * (math.log(0.1) - math.log(0.001)) + math.log(0.001))
    dt_bias = dt0 + jnp.log(-jnp.expm1(-dt0))
    a_log = jnp.log(jax.random.uniform(ks[6], (DEPTH, SSM_HEADS), f32, 1.0, 16.0))
    d_skip = 1.0 + normal(ks[7], (DEPTH, SSM_HEADS), 0.02)
    ssm_norm_w = 1.0 + normal(ks[8], (DEPTH, D_SSM), 0.02)
    attn_sinks = normal(ks[9], (DEPTH, ATTN_HEADS), 0.5)
    w_out = normal(ks[10], (DEPTH, D_MIX, D_MODEL), D_MIX ** -0.5)
    norm_ffn_w = 1.0 + normal(ks[11], (DEPTH, D_MODEL), 0.02)
    w_gate = normal(ks[12], (DEPTH, D_MODEL, D_FF), D_MODEL ** -0.5)
    w_up = normal(ks[13], (DEPTH, D_MODEL, D_FF), D_MODEL ** -0.5)
    w_down = normal(ks[14], (DEPTH, D_FF, D_MODEL), D_FF ** -0.5)
    norm_final_w = 1.0 + normal(ks[15], (D_MODEL,), 0.02)
    return {"x": x, "norm_mix_w": norm_mix_w, "w_in": w_in, "conv_w": conv_w,
            "conv_b": conv_b, "dt_bias": dt_bias, "a_log": a_log, "d_skip": d_skip,
            "ssm_norm_w": ssm_norm_w, "attn_sinks": attn_sinks, "w_out": w_out,
            "norm_ffn_w": norm_ffn_w, "w_gate": w_gate, "w_up": w_up,
            "w_down": w_down, "norm_final_w": norm_final_w}


def reference(x, norm_mix_w, w_in, conv_w, conv_b, dt_bias, a_log, d_skip,
              ssm_norm_w, attn_sinks, w_out, norm_ffn_w, w_gate, w_up, w_down,
              norm_final_w):
    for i in range(DEPTH):
        x = hybrid_layer(x, norm_mix_w[i], w_in[i], conv_w[i], conv_b[i], dt_bias[i],
                         a_log[i], d_skip[i], ssm_norm_w[i], attn_sinks[i], w_out[i],
                         norm_ffn_w[i], w_gate[i], w_up[i], w_down[i])
    return rms_norm(x, norm_final_w)
```

```python
import functools

import jax
import jax.numpy as jnp
from jax import lax
from jax.experimental import pallas as pl
from jax.experimental.pallas import tpu as pltpu

D_MODEL = 1024
SSM_HEADS = 8
SSM_HEAD_DIM = 64
D_SSM = SSM_HEADS * SSM_HEAD_DIM
SSM_GROUPS = 2
HEADS_PER_GROUP = SSM_HEADS // SSM_GROUPS
D_STATE = 128
CONV_WIDTH = 4
CHUNK = 128
D_BC = SSM_GROUPS * D_STATE
D_CONV = D_SSM + 2 * D_BC
ATTN_HEADS = 8
KV_HEADS = 2
Q_PER_KV = ATTN_HEADS // KV_HEADS
HEAD_DIM = 64
D_ATTN = ATTN_HEADS * HEAD_DIM
D_KV = KV_HEADS * HEAD_DIM
WINDOW = 128
D_MIX = D_SSM + D_ATTN
D_FF = 2816
EPS = 1e-5

LANES = 128
SUBLANES = 8
DT_PAD = LANES
COL_Z = 0
COL_XBC = COL_Z + D_SSM
COL_Q = COL_XBC + D_CONV
COL_K = COL_Q + D_ATTN
COL_V = COL_K + D_KV
COL_DT = COL_V + D_KV
D_PROJ = COL_DT + DT_PAD

TOKEN_TILE = 512
VMEM_LIMIT = 56 * 1024 * 1024


def _rms_norm(x, w):
    return x * lax.rsqrt(jnp.mean(x * x, axis=-1, keepdims=True) + EPS) * w


def _silu(x):
    return x * jax.nn.sigmoid(x)


def _bf16_dot(a, b):
    return jnp.dot(a.astype(jnp.bfloat16), b.astype(jnp.bfloat16),
                   preferred_element_type=jnp.float32)


def _bf16_dot_nt(a, b):
    return lax.dot_general(a.astype(jnp.bfloat16), b.astype(jnp.bfloat16),
                           (((1,), (1,)), ((), ())),
                           preferred_element_type=jnp.float32)


def _in_proj_kernel(x_ref, nw_ref, w_ref, z_ref, xbc_ref, q_ref, k_ref, v_ref, dt_ref):
    h = _rms_norm(x_ref[...], nw_ref[...])
    p = _bf16_dot(h, w_ref[...])
    z_ref[...] = p[:, COL_Z:COL_XBC].astype(z_ref.dtype)
    xbc_ref[...] = p[:, COL_XBC:COL_Q]
    q_ref[...] = p[:, COL_Q:COL_K].astype(q_ref.dtype)
    k_ref[...] = p[:, COL_K:COL_V].astype(k_ref.dtype)
    v_ref[...] = p[:, COL_V:COL_DT].astype(v_ref.dtype)
    dt_ref[...] = p[:, COL_DT:D_PROJ]


def _in_proj(x2d, norm_w, w_all):
    n_tok = x2d.shape[0]
    tm = TOKEN_TILE
    row = lambda i: (i, 0)
    const = lambda i: (0, 0)
    out_shapes = (
        jax.ShapeDtypeStruct((n_tok, D_SSM), jnp.bfloat16),
        jax.ShapeDtypeStruct((n_tok, D_CONV), jnp.float32),
        jax.ShapeDtypeStruct((n_tok, D_ATTN), jnp.bfloat16),
        jax.ShapeDtypeStruct((n_tok, D_KV), jnp.bfloat16),
        jax.ShapeDtypeStruct((n_tok, D_KV), jnp.bfloat16),
        jax.ShapeDtypeStruct((n_tok, DT_PAD), jnp.float32),
    )
    return pl.pallas_call(
        _in_proj_kernel,
        grid=(n_tok // tm,),
        in_specs=[
            pl.BlockSpec((tm, D_MODEL), row),
            pl.BlockSpec((1, D_MODEL), const),
            pl.BlockSpec((D_MODEL, D_PROJ), const),
        ],
        out_specs=tuple(pl.BlockSpec((tm, s.shape[1]), row) for s in out_shapes),
        out_shape=out_shapes,
        compiler_params=pltpu.CompilerParams(
            dimension_semantics=("arbitrary",), vmem_limit_bytes=VMEM_LIMIT),
        name="in_proj",
    )(x2d, norm_w, w_all)


def _expand_heads(vals, width):
    n = SSM_HEADS
    rows = vals.shape[0]
    lane_head = lax.broadcasted_iota(jnp.int32, (rows, n * width), 1) // width
    out = jnp.zeros((rows, n * width), jnp.float32)
    for h in range(n):
        out = jnp.where(lane_head == h, vals[:, h:h + 1], out)
    return out


def _mixer_kernel(sinks_ref, z_ref, xbc_ref, dt_ref, q_ref, k_ref, v_ref,
                  convw_ref, convb_ref, dtb_ref, alog_ref, dskip_ref, nw_ref,
                  o_ref, ext_ref, kprev_ref, vprev_ref, state_ref, y_ref):
    c = pl.program_id(1)

    @pl.when(c == 0)
    def _():
        ext_ref[0:SUBLANES, :] = jnp.zeros((SUBLANES, D_CONV), jnp.float32)
        kprev_ref[...] = jnp.zeros_like(kprev_ref)
        vprev_ref[...] = jnp.zeros_like(vprev_ref)
        state_ref[...] = jnp.zeros_like(state_ref)

    ext_ref[SUBLANES:SUBLANES + CHUNK, :] = xbc_ref[...]
    conv = convb_ref[...]
    for kk in range(CONV_WIDTH):
        lo = SUBLANES - (CONV_WIDTH - 1) + kk
        conv = conv + convw_ref[kk:kk + 1, :] * ext_ref[lo:lo + CHUNK, :]
    ext_ref[0:SUBLANES, :] = ext_ref[CHUNK:CHUNK + SUBLANES, :]
    xbc = _silu(conv)
    xs = xbc[:, :D_SSM]
    b_all = xbc[:, D_SSM:D_SSM + D_BC]
    c_all = xbc[:, D_SSM + D_BC:]

    dt = jax.nn.softplus(dt_ref[...] + dtb_ref[...])
    a = -jnp.exp(alog_ref[...])
    a_cum = dt * a
    row = lax.broadcasted_iota(jnp.int32, (CHUNK, DT_PAD), 0)
    shift = 1
    while shift < CHUNK:
        a_cum = a_cum + jnp.where(row >= shift, pltpu.roll(a_cum, shift, 0), 0.0)
        shift *= 2
    a_cum_t = a_cum.T
    a_last = a_cum[CHUNK - 1:CHUNK, :]
    decay_to_end = jnp.exp(a_last - a_cum)
    decay_from_start = jnp.exp(a_cum)
    chunk_decay = jnp.exp(a_last)

    xdt = xs * _expand_heads(dt, SSM_HEAD_DIM)
    xdt_end = xdt * _expand_heads(decay_to_end, SSM_HEAD_DIM)
    off_scale = _expand_heads(decay_from_start, SSM_HEAD_DIM)
    state_scale = _expand_heads(chunk_decay, SSM_HEAD_DIM)

    t_idx = lax.broadcasted_iota(jnp.int32, (CHUNK, CHUNK), 0)
    s_idx = lax.broadcasted_iota(jnp.int32, (CHUNK, CHUNK), 1)
    causal = t_idx >= s_idx
    gw = HEADS_PER_GROUP * SSM_HEAD_DIM
    for g in range(SSM_GROUPS):
        b_g = b_all[:, g * D_STATE:(g + 1) * D_STATE]
        c_g = c_all[:, g * D_STATE:(g + 1) * D_STATE]
        cb = _bf16_dot_nt(c_g, b_g)
        state = state_ref[g]
        y_off = _bf16_dot(c_g, state) * off_scale[:, g * gw:(g + 1) * gw]
        new_states = _bf16_dot(b_g.T, xdt_end[:, g * gw:(g + 1) * gw])
        state_ref[g] = state * state_scale[:, g * gw:(g + 1) * gw] + new_states
        for r in range(HEADS_PER_GROUP):
            h = g * HEADS_PER_GROUP + r
            seg = a_cum[:, h:h + 1] - a_cum_t[h:h + 1, :]
            decay = jnp.exp(jnp.where(causal, seg, -jnp.inf))
            lo = h * SSM_HEAD_DIM
            y_diag = _bf16_dot(cb * decay, xdt[:, lo:lo + SSM_HEAD_DIM])
            y_ref[:, lo:lo + SSM_HEAD_DIM] = (
                y_diag + y_off[:, r * SSM_HEAD_DIM:(r + 1) * SSM_HEAD_DIM])

    y = y_ref[...] + dskip_ref[...] * xs
    y = y * _silu(z_ref[...].astype(jnp.float32))
    for g in range(SSM_GROUPS):
        yg = y[:, g * gw:(g + 1) * gw]
        o_ref[:, g * gw:(g + 1) * gw] = _rms_norm(
            yg, nw_ref[:, g * gw:(g + 1) * gw]).astype(o_ref.dtype)

    q_pos = lax.broadcasted_iota(jnp.int32, (CHUNK, 2 * WINDOW), 0) + WINDOW
    k_pos = lax.broadcasted_iota(jnp.int32, (CHUNK, 2 * WINDOW), 1)
    band = (k_pos <= q_pos) & (k_pos > q_pos - WINDOW)
    mask = band & ((k_pos >= WINDOW) | (c > 0))
    k_cur = k_ref[...]
    v_cur = v_ref[...]
    k_ext = jnp.concatenate([kprev_ref[...], k_cur], axis=0)
    v_ext = jnp.concatenate([vprev_ref[...], v_cur], axis=0)
    kprev_ref[...] = k_cur
    vprev_ref[...] = v_cur
    scale = HEAD_DIM ** -0.5
    for j in range(KV_HEADS):
        k_j = k_ext[:, j * HEAD_DIM:(j + 1) * HEAD_DIM]
        v_j = v_ext[:, j * HEAD_DIM:(j + 1) * HEAD_DIM]
        for r in range(Q_PER_KV):
            h = j * Q_PER_KV + r
            q_h = q_ref[:, h * HEAD_DIM:(h + 1) * HEAD_DIM]
            scores = _bf16_dot_nt(q_h, k_j) * scale
            scores = jnp.where(mask, scores, -jnp.inf)
            sink = sinks_ref[h]
            m = jnp.maximum(jnp.max(scores, axis=-1, keepdims=True), sink)
            p = jnp.exp(scores - m)
            denom = jnp.sum(p, axis=-1, keepdims=True) + jnp.exp(sink - m)
            out = _bf16_dot(p, v_j) / denom
            lo = D_SSM + h * HEAD_DIM
            o_ref[:, lo:lo + HEAD_DIM] = out.astype(o_ref.dtype)


def _mixer(z, xbc, dt, q, k, v, conv_w, conv_b, dt_bias, a_log, d_skip, ssm_norm_w,
           sinks, bsz, seq):
    nc = seq // CHUNK
    z, xbc, dt, q, k, v = (t.reshape(bsz, seq, t.shape[-1]) for t in (z, xbc, dt, q, k, v))
    chunked = lambda width: pl.BlockSpec((None, CHUNK, width), lambda b, c: (b, c, 0))
    const = lambda shape: pl.BlockSpec(shape, lambda b, c: (0, 0))
    gw = HEADS_PER_GROUP * SSM_HEAD_DIM
    return pl.pallas_call(
        _mixer_kernel,
        grid=(bsz, nc),
        in_specs=[
            pl.BlockSpec(memory_space=pltpu.SMEM),
            chunked(D_SSM), chunked(D_CONV), chunked(DT_PAD),
            chunked(D_ATTN), chunked(D_KV), chunked(D_KV),
            const((CONV_WIDTH, D_CONV)), const((1, D_CONV)), const((1, DT_PAD)),
            const((1, DT_PAD)), const((1, D_SSM)), const((1, D_SSM)),
        ],
        out_specs=chunked(D_MIX),
        out_shape=jax.ShapeDtypeStruct((bsz, seq, D_MIX), jnp.bfloat16),
        scratch_shapes=[
            pltpu.VMEM((SUBLANES + CHUNK, D_CONV), jnp.float32),
            pltpu.VMEM((WINDOW, D_KV), jnp.bfloat16),
            pltpu.VMEM((WINDOW, D_KV), jnp.bfloat16),
            pltpu.VMEM((SSM_GROUPS, D_STATE, gw), jnp.float32),
            pltpu.VMEM((CHUNK, D_SSM), jnp.float32),
        ],
        compiler_params=pltpu.CompilerParams(
            dimension_semantics=("arbitrary", "arbitrary"), vmem_limit_bytes=VMEM_LIMIT),
        name="mixer",
    )(sinks, z, xbc, dt, q, k, v, conv_w, conv_b, dt_bias, a_log, d_skip, ssm_norm_w)


def _out_ffn_kernel(x_ref, y_ref, wo_ref, nffn_ref, wg_ref, wu_ref, wd_ref, nfin_ref, o_ref):
    x1 = x_ref[...] + jnp.dot(y_ref[...], wo_ref[...], preferred_element_type=jnp.float32)
    h = _rms_norm(x1, nffn_ref[...]).astype(jnp.bfloat16)
    gate = jnp.dot(h, wg_ref[...], preferred_element_type=jnp.float32)
    up = jnp.dot(h, wu_ref[...], preferred_element_type=jnp.float32)
    act = (_silu(gate) * up).astype(jnp.bfloat16)
    x2 = x1 + jnp.dot(act, wd_ref[...], preferred_element_type=jnp.float32)
    o_ref[...] = _rms_norm(x2, nfin_ref[...])


def _out_ffn(x2d, y2d, w_out, norm_ffn_w, w_gate, w_up, w_down, norm_final_w):
    n_tok = x2d.shape[0]
    tm = TOKEN_TILE
    row = lambda i: (i, 0)
    resident = lambda shape: pl.BlockSpec(shape, lambda i: (0, 0), pipeline_mode=pl.Buffered(1))
    return pl.pallas_call(
        _out_ffn_kernel,
        grid=(n_tok // tm,),
        in_specs=[
            pl.BlockSpec((tm, D_MODEL), row),
            pl.BlockSpec((tm, D_MIX), row),
            resident((D_MIX, D_MODEL)),
            resident((1, D_MODEL)),
            resident((D_MODEL, D_FF)),
            resident((D_MODEL, D_FF)),
            resident((D_FF, D_MODEL)),
            resident((1, D_MODEL)),
        ],
        out_specs=pl.BlockSpec((tm, D_MODEL), row),
        out_shape=jax.ShapeDtypeStruct((n_tok, D_MODEL), jnp.float32),
        compiler_params=pltpu.CompilerParams(
            dimension_semantics=("arbitrary",), vmem_limit_bytes=VMEM_LIMIT),
        name="out_ffn",
    )(x2d, y2d, w_out, norm_ffn_w, w_gate, w_up, w_down, norm_final_w)


def _pad_lanes(v, width):
    return jnp.pad(v, (0, width - v.shape[0])).reshape(1, width)


def _layer(x, norm_mix_w, w_in, conv_w, conv_b, dt_bias, a_log, d_skip, ssm_norm_w,
           attn_sinks, w_out, norm_ffn_w, w_gate, w_up, w_down, norm_out_w):
    bsz, seq, _ = x.shape
    bf16 = jnp.bfloat16
    s_xbc = D_SSM + D_CONV
    s_dt = s_xbc + SSM_HEADS
    w_all = jnp.concatenate(
        [w_in[:, :s_xbc], w_in[:, s_dt:], w_in[:, s_xbc:s_dt],
         jnp.zeros((D_MODEL, DT_PAD - SSM_HEADS), w_in.dtype)], axis=1).astype(bf16)
    x2d = x.reshape(bsz * seq, D_MODEL)
    z, xbc, q, k, v, dt = _in_proj(x2d, norm_mix_w.reshape(1, D_MODEL), w_all)
    y = _mixer(z, xbc, dt, q, k, v, conv_w, conv_b.reshape(1, D_CONV),
               _pad_lanes(dt_bias, DT_PAD), _pad_lanes(a_log, DT_PAD),
               jnp.repeat(d_skip, SSM_HEAD_DIM).reshape(1, D_SSM),
               ssm_norm_w.reshape(1, D_SSM), attn_sinks, bsz, seq)
    out = _out_ffn(x2d, y.reshape(bsz * seq, D_MIX), w_out.astype(bf16),
                   norm_ffn_w.reshape(1, D_MODEL), w_gate.astype(bf16), w_up.astype(bf16),
                   w_down.astype(bf16), norm_out_w.reshape(1, D_MODEL))
    return out.reshape(bsz, seq, D_MODEL)


def kernel(x, norm_mix_w, w_in, conv_w, conv_b, dt_bias, a_log, d_skip, ssm_norm_w,
           attn_sinks, w_out, norm_ffn_w, w_gate, w_up, w_down, norm_final_w):
    depth = norm_mix_w.shape[0]
    assert depth == 1, "final RMSNorm is fused into the single layer's FFN kernel"
    return _layer(x, norm_mix_w[0], w_in[0], conv_w[0], conv_b[0], dt_bias[0], a_log[0],
                  d_skip[0], ssm_norm_w[0], attn_sinks[0], w_out[0], norm_ffn_w[0],
                  w_gate[0], w_up[0], w_down[0], norm_final_w)
```

```python
import functools

import jax
import jax.numpy as jnp
from jax import lax
from jax.experimental import pallas as pl
from jax.experimental.pallas import tpu as pltpu

D_MODEL = 1024
SSM_HEADS = 8
SSM_HEAD_DIM = 64
D_SSM = SSM_HEADS * SSM_HEAD_DIM
SSM_GROUPS = 2
HEADS_PER_GROUP = SSM_HEADS // SSM_GROUPS
D_STATE = 128
CONV_WIDTH = 4
CHUNK = 128
D_BC = SSM_GROUPS * D_STATE
D_CONV = D_SSM + 2 * D_BC
ATTN_HEADS = 8
KV_HEADS = 2
Q_PER_KV = ATTN_HEADS // KV_HEADS
HEAD_DIM = 64
D_ATTN = ATTN_HEADS * HEAD_DIM
D_KV = KV_HEADS * HEAD_DIM
WINDOW = 128
D_MIX = D_SSM + D_ATTN
D_FF = 2816
EPS = 1e-5

LANES = 128
SUBLANES = 8
DT_PAD = LANES
COL_Z = 0
COL_XBC = COL_Z + D_SSM
COL_Q = COL_XBC + D_CONV
COL_K = COL_Q + D_ATTN
COL_V = COL_K + D_KV
COL_DT = COL_V + D_KV
D_PROJ = COL_DT + DT_PAD

TOKEN_TILE = 512
VMEM_LIMIT = 56 * 1024 * 1024


def _rms_norm(x, w):
    return x * lax.rsqrt(jnp.mean(x * x, axis=-1, keepdims=True) + EPS) * w


def _silu(x):
    return x * jax.nn.sigmoid(x)


def _bf16_dot(a, b):
    return jnp.dot(a.astype(jnp.bfloat16), b.astype(jnp.bfloat16),
                   preferred_element_type=jnp.float32)


def _bf16_dot_nt(a, b):
    return lax.dot_general(a.astype(jnp.bfloat16), b.astype(jnp.bfloat16),
                           (((1,), (1,)), ((), ())),
                           preferred_element_type=jnp.float32)


def _in_proj_kernel(x_ref, nw_ref, w_ref, z_ref, xbc_ref, q_ref, k_ref, v_ref, dt_ref):
    h = _rms_norm(x_ref[...], nw_ref[...])
    p = _bf16_dot(h, w_ref[...])
    z_ref[...] = p[:, COL_Z:COL_XBC].astype(z_ref.dtype)
    xbc_ref[...] = p[:, COL_XBC:COL_Q]
    q_ref[...] = p[:, COL_Q:COL_K].astype(q_ref.dtype)
    k_ref[...] = p[:, COL_K:COL_V].astype(k_ref.dtype)
    v_ref[...] = p[:, COL_V:COL_DT].astype(v_ref.dtype)
    dt_ref[...] = p[:, COL_DT:D_PROJ]


def _in_proj(x2d, norm_w, w_all):
    n_tok = x2d.shape[0]
    tm = TOKEN_TILE
    row = lambda i: (i, 0)
    const = lambda i: (0, 0)
    out_shapes = (
        jax.ShapeDtypeStruct((n_tok, D_SSM), jnp.bfloat16),
        jax.ShapeDtypeStruct((n_tok, D_CONV), jnp.float32),
        jax.ShapeDtypeStruct((n_tok, D_ATTN), jnp.bfloat16),
        jax.ShapeDtypeStruct((n_tok, D_KV), jnp.bfloat16),
        jax.ShapeDtypeStruct((n_tok, D_KV), jnp.bfloat16),
        jax.ShapeDtypeStruct((n_tok, DT_PAD), jnp.float32),
    )
    return pl.pallas_call(
        _in_proj_kernel,
        grid=(n_tok // tm,),
        in_specs=[
            pl.BlockSpec((tm, D_MODEL), row),
            pl.BlockSpec((1, D_MODEL), const),
            pl.BlockSpec((D_MODEL, D_PROJ), const),
        ],
        out_specs=tuple(pl.BlockSpec((tm, s.shape[1]), row) for s in out_shapes),
        out_shape=out_shapes,
        compiler_params=pltpu.CompilerParams(
            dimension_semantics=("arbitrary",), vmem_limit_bytes=VMEM_LIMIT),
        name="in_proj",
    )(x2d, norm_w, w_all)


def _expand_heads(vals, width):
    n = SSM_HEADS
    rows = vals.shape[0]
    lane_head = lax.broadcasted_iota(jnp.int32, (rows, n * width), 1) // width
    out = jnp.zeros((rows, n * width), jnp.float32)
    for h in range(n):
        out = jnp.where(lane_head == h, vals[:, h:h + 1], out)
    return out


def _mixer_reset(ext_ref, kprev_ref, vprev_ref, state_ref):
    ext_ref[0:SUBLANES, :] = jnp.zeros((SUBLANES, D_CONV), jnp.float32)
    kprev_ref[...] = jnp.zeros_like(kprev_ref)
    vprev_ref[...] = jnp.zeros_like(vprev_ref)
    state_ref[...] = jnp.zeros_like(state_ref)


def _mixer_body(c, sinks_ref, z_ref, xbc_ref, dt_ref, q_ref, k_ref, v_ref,
                convw_ref, convb_ref, dtb_ref, alog_ref, dskip_ref, nw_ref,
                o_ref, ext_ref, kprev_ref, vprev_ref, state_ref, y_ref):
    ext_ref[SUBLANES:SUBLANES + CHUNK, :] = xbc_ref[...]
    conv = convb_ref[...]
    for kk in range(CONV_WIDTH):
        lo = SUBLANES - (CONV_WIDTH - 1) + kk
        conv = conv + convw_ref[kk:kk + 1, :] * ext_ref[lo:lo + CHUNK, :]
    ext_ref[0:SUBLANES, :] = ext_ref[CHUNK:CHUNK + SUBLANES, :]
    xbc = _silu(conv)
    xs = xbc[:, :D_SSM]
    b_all = xbc[:, D_SSM:D_SSM + D_BC]
    c_all = xbc[:, D_SSM + D_BC:]

    dt = jax.nn.softplus(dt_ref[...] + dtb_ref[...])
    a = -jnp.exp(alog_ref[...])
    a_cum = dt * a
    row = lax.broadcasted_iota(jnp.int32, (CHUNK, DT_PAD), 0)
    shift = 1
    while shift < CHUNK:
        a_cum = a_cum + jnp.where(row >= shift, pltpu.roll(a_cum, shift, 0), 0.0)
        shift *= 2
    a_cum_t = a_cum.T
    a_last = a_cum[CHUNK - 1:CHUNK, :]
    decay_to_end = jnp.exp(a_last - a_cum)
    decay_from_start = jnp.exp(a_cum)
    chunk_decay = jnp.exp(a_last)

    xdt = xs * _expand_heads(dt, SSM_HEAD_DIM)
    xdt_end = xdt * _expand_heads(decay_to_end, SSM_HEAD_DIM)
    off_scale = _expand_heads(decay_from_start, SSM_HEAD_DIM)
    state_scale = _expand_heads(chunk_decay, SSM_HEAD_DIM)

    t_idx = lax.broadcasted_iota(jnp.int32, (CHUNK, CHUNK), 0)
    s_idx = lax.broadcasted_iota(jnp.int32, (CHUNK, CHUNK), 1)
    causal = t_idx >= s_idx
    gw = HEADS_PER_GROUP * SSM_HEAD_DIM
    for g in range(SSM_GROUPS):
        b_g = b_all[:, g * D_STATE:(g + 1) * D_STATE]
        c_g = c_all[:, g * D_STATE:(g + 1) * D_STATE]
        cb = _bf16_dot_nt(c_g, b_g)
        state = state_ref[g]
        y_off = _bf16_dot(c_g, state) * off_scale[:, g * gw:(g + 1) * gw]
        new_states = _bf16_dot(b_g.T, xdt_end[:, g * gw:(g + 1) * gw])
        state_ref[g] = state * state_scale[:, g * gw:(g + 1) * gw] + new_states
        for r in range(HEADS_PER_GROUP):
            h = g * HEADS_PER_GROUP + r
            seg = a_cum[:, h:h + 1] - a_cum_t[h:h + 1, :]
            decay = jnp.exp(jnp.where(causal, seg, -jnp.inf))
            lo = h * SSM_HEAD_DIM
            y_diag = _bf16_dot(cb * decay, xdt[:, lo:lo + SSM_HEAD_DIM])
            y_ref[:, lo:lo + SSM_HEAD_DIM] = (
                y_diag + y_off[:, r * SSM_HEAD_DIM:(r + 1) * SSM_HEAD_DIM])

    y = y_ref[...] + dskip_ref[...] * xs
    y = y * _silu(z_ref[...].astype(jnp.float32))
    for g in range(SSM_GROUPS):
        yg = y[:, g * gw:(g + 1) * gw]
        o_ref[:, g * gw:(g + 1) * gw] = _rms_norm(
            yg, nw_ref[:, g * gw:(g + 1) * gw]).astype(o_ref.dtype)

    q_pos = lax.broadcasted_iota(jnp.int32, (CHUNK, 2 * WINDOW), 0) + WINDOW
    k_pos = lax.broadcasted_iota(jnp.int32, (CHUNK, 2 * WINDOW), 1)
    band = (k_pos <= q_pos) & (k_pos > q_pos - WINDOW)
    mask = band & ((k_pos >= WINDOW) | (c > 0))
    k_cur = k_ref[...]
    v_cur = v_ref[...]
    k_ext = jnp.concatenate([kprev_ref[...], k_cur], axis=0)
    v_ext = jnp.concatenate([vprev_ref[...], v_cur], axis=0)
    kprev_ref[...] = k_cur
    vprev_ref[...] = v_cur
    scale = HEAD_DIM ** -0.5
    for j in range(KV_HEADS):
        k_j = k_ext[:, j * HEAD_DIM:(j + 1) * HEAD_DIM]
        v_j = v_ext[:, j * HEAD_DIM:(j + 1) * HEAD_DIM]
        for r in range(Q_PER_KV):
            h = j * Q_PER_KV + r
            q_h = q_ref[:, h * HEAD_DIM:(h + 1) * HEAD_DIM]
            scores = _bf16_dot_nt(q_h, k_j) * scale
            scores = jnp.where(mask, scores, -jnp.inf)
            sink = sinks_ref[h]
            m = jnp.maximum(jnp.max(scores, axis=-1, keepdims=True), sink)
            p = jnp.exp(scores - m)
            denom = jnp.sum(p, axis=-1, keepdims=True) + jnp.exp(sink - m)
            out = _bf16_dot(p, v_j) / denom
            lo = D_SSM + h * HEAD_DIM
            o_ref[:, lo:lo + HEAD_DIM] = out.astype(o_ref.dtype)


FFN_SKEW = 3


def _stage_out_proj(x_ref, ymix_ref, wo_ref, nffn_ref, x1_ref, h_ref):
    x1 = x_ref[...] + jnp.dot(ymix_ref[...], wo_ref[...], preferred_element_type=jnp.float32)
    x1_ref[...] = x1
    h_ref[...] = _rms_norm(x1, nffn_ref[...]).astype(h_ref.dtype)


def _stage_gate_up(h_ref, wg_ref, wu_ref, act_ref, x1_in_ref, x1_out_ref):
    h = h_ref[...]
    gate = jnp.dot(h, wg_ref[...], preferred_element_type=jnp.float32)
    up = jnp.dot(h, wu_ref[...], preferred_element_type=jnp.float32)
    act_ref[...] = (_silu(gate) * up).astype(act_ref.dtype)
    x1_out_ref[...] = x1_in_ref[...]


def _stage_down(act_ref, wd_ref, x1_ref, nfin_ref, o_ref):
    x2 = x1_ref[...] + jnp.dot(act_ref[...], wd_ref[...], preferred_element_type=jnp.float32)
    o_ref[...] = _rms_norm(x2, nfin_ref[...])


def _mixer_ffn_kernel(nc, sinks_ref, z_ref, xbc_ref, dt_ref, q_ref, k_ref, v_ref,
                      convw_ref, convb_ref, dtb_ref, alog_ref, dskip_ref, nw_ref,
                      x_ref, wo_ref, nffn_ref, wg_ref, wu_ref, wd_ref, nfin_ref,
                      o_ref, ymix_ref, h_ref, x1a_ref, x1b_ref, act_ref,
                      ext_ref, kprev_ref, vprev_ref, state_ref, y_ref):
    s = pl.program_id(0)
    c = lax.rem(s, nc)

    @pl.when(s == 0)
    def _():
        for ref in (ymix_ref, h_ref, x1a_ref, x1b_ref, act_ref):
            ref[...] = jnp.zeros_like(ref)

    @pl.when(c == 0)
    def _():
        _mixer_reset(ext_ref, kprev_ref, vprev_ref, state_ref)

    def step(wr):
        rd = 1 - wr
        _stage_down(act_ref.at[rd], wd_ref, x1b_ref.at[rd], nfin_ref, o_ref)
        _stage_gate_up(h_ref.at[rd], wg_ref, wu_ref, act_ref.at[wr], x1a_ref.at[rd],
                       x1b_ref.at[wr])
        _stage_out_proj(x_ref, ymix_ref.at[rd], wo_ref, nffn_ref, x1a_ref.at[wr], h_ref.at[wr])
        _mixer_body(c, sinks_ref, z_ref, xbc_ref, dt_ref, q_ref, k_ref, v_ref,
                    convw_ref, convb_ref, dtb_ref, alog_ref, dskip_ref, nw_ref,
                    ymix_ref.at[wr], ext_ref, kprev_ref, vprev_ref, state_ref, y_ref)

    parity = lax.rem(s, 2)
    for wr in range(2):
        pl.when(parity == wr)(functools.partial(step, wr))


def _mixer_ffn(z, xbc, dt, q, k, v, conv_w, conv_b, dt_bias, a_log, d_skip, ssm_norm_w,
               sinks, x2d, w_out, norm_ffn_w, w_gate, w_up, w_down, norm_final_w, seq):
    n_tok = x2d.shape[0]
    n_chunks = n_tok // CHUNK
    nc = seq // CHUNK
    clamp = lambda i: jnp.clip(i, 0, n_chunks - 1)
    blk = lambda width, lag: pl.BlockSpec((CHUNK, width), lambda s: (clamp(s - lag), 0))
    const = lambda shape: pl.BlockSpec(shape, lambda s: (0, 0))
    resident = lambda shape: pl.BlockSpec(shape, lambda s: (0, 0), pipeline_mode=pl.Buffered(1))
    gw = HEADS_PER_GROUP * SSM_HEAD_DIM
    return pl.pallas_call(
        functools.partial(_mixer_ffn_kernel, nc),
        grid=(n_chunks + FFN_SKEW,),
        in_specs=[
            pl.BlockSpec(memory_space=pltpu.SMEM),
            blk(D_SSM, 0), blk(D_CONV, 0), blk(DT_PAD, 0),
            blk(D_ATTN, 0), blk(D_KV, 0), blk(D_KV, 0),
            const((CONV_WIDTH, D_CONV)), const((1, D_CONV)), const((1, DT_PAD)),
            const((1, DT_PAD)), const((1, D_SSM)), const((1, D_SSM)),
            blk(D_MODEL, 1),
            resident((D_MIX, D_MODEL)),
            resident((1, D_MODEL)),
            resident((D_MODEL, D_FF)),
            resident((D_MODEL, D_FF)),
            resident((D_FF, D_MODEL)),
            resident((1, D_MODEL)),
        ],
        out_specs=blk(D_MODEL, FFN_SKEW),
        out_shape=jax.ShapeDtypeStruct((n_tok, D_MODEL), jnp.float32),
        scratch_shapes=[
            pltpu.VMEM((2, CHUNK, D_MIX), jnp.bfloat16),
            pltpu.VMEM((2, CHUNK, D_MODEL), jnp.bfloat16),
            pltpu.VMEM((2, CHUNK, D_MODEL), jnp.float32),
            pltpu.VMEM((2, CHUNK, D_MODEL), jnp.float32),
            pltpu.VMEM((2, CHUNK, D_FF), jnp.bfloat16),
            pltpu.VMEM((SUBLANES + CHUNK, D_CONV), jnp.float32),
            pltpu.VMEM((WINDOW, D_KV), jnp.bfloat16),
            pltpu.VMEM((WINDOW, D_KV), jnp.bfloat16),
            pltpu.VMEM((SSM_GROUPS, D_STATE, gw), jnp.float32),
            pltpu.VMEM((CHUNK, D_SSM), jnp.float32),
        ],
        compiler_params=pltpu.CompilerParams(
            dimension_semantics=("arbitrary",), vmem_limit_bytes=VMEM_LIMIT),
        name="mixer_ffn",
    )(sinks, z, xbc, dt, q, k, v, conv_w, conv_b, dt_bias, a_log, d_skip, ssm_norm_w,
      x2d, w_out, norm_ffn_w, w_gate, w_up, w_down, norm_final_w)


def _pad_lanes(v, width):
    return jnp.pad(v, (0, width - v.shape[0])).reshape(1, width)


def _layer(x, norm_mix_w, w_in, conv_w, conv_b, dt_bias, a_log, d_skip, ssm_norm_w,
           attn_sinks, w_out, norm_ffn_w, w_gate, w_up, w_down, norm_out_w):
    bsz, seq, _ = x.shape
    bf16 = jnp.bfloat16
    s_xbc = D_SSM + D_CONV
    s_dt = s_xbc + SSM_HEADS
    w_all = jnp.concatenate(
        [w_in[:, :s_xbc], w_in[:, s_dt:], w_in[:, s_xbc:s_dt],
         jnp.zeros((D_MODEL, DT_PAD - SSM_HEADS), w_in.dtype)], axis=1).astype(bf16)
    x2d = x.reshape(bsz * seq, D_MODEL)
    z, xbc, q, k, v, dt = _in_proj(x2d, norm_mix_w.reshape(1, D_MODEL), w_all)
    out = _mixer_ffn(z, xbc, dt, q, k, v, conv_w, conv_b.reshape(1, D_CONV),
                     _pad_lanes(dt_bias, DT_PAD), _pad_lanes(a_log, DT_PAD),
                     jnp.repeat(d_skip, SSM_HEAD_DIM).reshape(1, D_SSM),
                     ssm_norm_w.reshape(1, D_SSM), attn_sinks, x2d, w_out.astype(bf16),
                     norm_ffn_w.reshape(1, D_MODEL), w_gate.astype(bf16), w_up.astype(bf16),
                     w_down.astype(bf16), norm_out_w.reshape(1, D_MODEL), seq)
    return out.reshape(bsz, seq, D_MODEL)


def kernel(x, norm_mix_w, w_in, conv_w, conv_b, dt_bias, a_log, d_skip, ssm_norm_w,
           attn_sinks, w_out, norm_ffn_w, w_gate, w_up, w_down, norm_final_w):
    depth = norm_mix_w.shape[0]
    assert depth == 1, "final RMSNorm is fused into the single layer's FFN kernel"
    return _layer(x, norm_mix_w[0], w_in[0], conv_w[0], conv_b[0], dt_bias[0], a_log[0],
                  d_skip[0], ssm_norm_w[0], attn_sinks[0], w_out[0], norm_ffn_w[0],
                  w_gate[0], w_up[0], w_down[0], norm_final_w)
```

```python
import math

import jax
import jax.numpy as jnp
from jax import lax
from jax.experimental import pallas as pl
from jax.experimental.pallas import tpu as pltpu

D_MODEL = 1024
SSM_HEADS = 8
SSM_HEAD_DIM = 64
D_SSM = SSM_HEADS * SSM_HEAD_DIM
SSM_GROUPS = 2
HEADS_PER_GROUP = SSM_HEADS // SSM_GROUPS
D_STATE = 128
CONV_WIDTH = 4
CHUNK = 128
D_BC = SSM_GROUPS * D_STATE
D_CONV = D_SSM + 2 * D_BC
ATTN_HEADS = 8
KV_HEADS = 2
Q_PER_KV = ATTN_HEADS // KV_HEADS
HEAD_DIM = 64
D_ATTN = ATTN_HEADS * HEAD_DIM
D_KV = KV_HEADS * HEAD_DIM
WINDOW = 128
D_MIX = D_SSM + D_ATTN
D_FF = 2816
EPS = 1e-5

LANES = 128
HEAD_PAIR = LANES // HEAD_DIM
DT_PAD = LANES
LOG2E = math.log2(math.e)
MASKED = -1e30
COL_Z = 0
COL_XBC = COL_Z + D_SSM
COL_Q = COL_XBC + D_CONV
COL_K = COL_Q + D_ATTN
COL_V = COL_K + D_KV
COL_DT = COL_V + D_KV
D_PROJ = COL_DT + DT_PAD

TOKEN_TILE = 512
ROWS_PER_STEP = 2
VMEM_LIMIT = 56 * 1024 * 1024

assert CHUNK == WINDOW == D_STATE == LANES and HEAD_DIM == SSM_HEAD_DIM and HEAD_PAIR == 2


def _rms_norm(x, w):
    return x * lax.rsqrt(jnp.mean(x * x, axis=-1, keepdims=True) + EPS) * w


def _silu(x):
    return x * jax.nn.sigmoid(x)


def _bf16_dot(a, b):
    return jnp.dot(a.astype(jnp.bfloat16), b.astype(jnp.bfloat16),
                   preferred_element_type=jnp.float32)


def _bf16_dot_nt(a, b):
    return lax.dot_general(a.astype(jnp.bfloat16), b.astype(jnp.bfloat16),
                           (((1,), (1,)), ((), ())),
                           preferred_element_type=jnp.float32)


def _in_proj_kernel(x_ref, nw_ref, w_ref, z_ref, xbc_ref, q_ref, k_ref, v_ref, dt_ref):
    h = _rms_norm(x_ref[...], nw_ref[...])
    p = _bf16_dot(h, w_ref[...])
    z_ref[...] = p[:, COL_Z:COL_XBC].astype(z_ref.dtype)
    xbc_ref[...] = p[:, COL_XBC:COL_Q].astype(xbc_ref.dtype)
    q_ref[...] = p[:, COL_Q:COL_K].astype(q_ref.dtype)
    k_ref[...] = p[:, COL_K:COL_V].astype(k_ref.dtype)
    v_ref[...] = p[:, COL_V:COL_DT].astype(v_ref.dtype)
    dt_ref[...] = p[:, COL_DT:D_PROJ]


def _in_proj(x2d, norm_w, w_all):
    n_tok = x2d.shape[0]
    tm = TOKEN_TILE
    row = lambda i: (i, 0)
    const = lambda i: (0, 0)
    out_shapes = (
        jax.ShapeDtypeStruct((n_tok, D_SSM), jnp.bfloat16),
        jax.ShapeDtypeStruct((n_tok, D_CONV), jnp.bfloat16),
        jax.ShapeDtypeStruct((n_tok, D_ATTN), jnp.bfloat16),
        jax.ShapeDtypeStruct((n_tok, D_KV), jnp.bfloat16),
        jax.ShapeDtypeStruct((n_tok, D_KV), jnp.bfloat16),
        jax.ShapeDtypeStruct((n_tok, DT_PAD), jnp.float32),
    )
    return pl.pallas_call(
        _in_proj_kernel,
        grid=(n_tok // tm,),
        in_specs=[
            pl.BlockSpec((tm, D_MODEL), row),
            pl.BlockSpec((1, D_MODEL), const),
            pl.BlockSpec((D_MODEL, D_PROJ), const),
        ],
        out_specs=tuple(pl.BlockSpec((tm, s.shape[1]), row) for s in out_shapes),
        out_shape=out_shapes,
        compiler_params=pltpu.CompilerParams(
            dimension_semantics=("arbitrary",), vmem_limit_bytes=VMEM_LIMIT),
        name="in_proj",
    )(x2d, norm_w, w_all)


def _expand_heads(vals):
    rows = vals.shape[0]
    low_half = lax.broadcasted_iota(jnp.int32, (rows, LANES), 1) < SSM_HEAD_DIM
    tiles = [jnp.where(low_half, vals[:, HEAD_PAIR * j:HEAD_PAIR * j + 1],
                       vals[:, HEAD_PAIR * j + 1:HEAD_PAIR * j + 2])
             for j in range(SSM_HEADS // HEAD_PAIR)]
    return jnp.concatenate(tiles, axis=1)


def _split_halves(x, low_half):
    zero = jnp.zeros_like(x)
    return jnp.concatenate([jnp.where(low_half, x, zero), jnp.where(low_half, zero, x)], axis=0)


def _run_interleaved(stage_generators):
    pending = list(stage_generators)
    while pending:
        for gen in list(pending):
            try:
                next(gen)
            except StopIteration:
                pending.remove(gen)


def _mixer_kernel(sinks_ref, z_ref, xbc_ref, dt_ref, q_ref, k_ref, v_ref,
                  shift_ref, eye_ref, bias_ref,
                  convw_ref, convb_ref, dtb_ref, alog_ref, dskip_ref, nw_ref,
                  o_ref, ext_ref, kext_ref, vext_ref, state_ref):
    c = pl.program_id(1)

    @pl.when(c == 0)
    def _():
        ext_ref[:, 0:CHUNK, :] = jnp.zeros((ROWS_PER_STEP, CHUNK, D_CONV), ext_ref.dtype)
        kext_ref[:, 0:WINDOW, :] = jnp.zeros((ROWS_PER_STEP, WINDOW, D_KV), kext_ref.dtype)
        vext_ref[:, 0:WINDOW, :] = jnp.zeros((ROWS_PER_STEP, WINDOW, D_KV), vext_ref.dtype)
        state_ref[...] = jnp.zeros_like(state_ref)

    stages = []
    for r in range(ROWS_PER_STEP):
        stages.append(_attention_stages(
            c, sinks_ref, q_ref.at[r], k_ref.at[r], v_ref.at[r], eye_ref, bias_ref,
            o_ref.at[r], kext_ref.at[r], vext_ref.at[r]))
        stages.append(_ssd_stages(
            z_ref.at[r], xbc_ref.at[r], dt_ref.at[r], shift_ref, convw_ref, convb_ref,
            dtb_ref, alog_ref, dskip_ref, nw_ref, o_ref.at[r], ext_ref.at[r], state_ref.at[r]))
    _run_interleaved(stages)


def _ssd_stages(z_ref, xbc_ref, dt_ref, shift_ref, convw_ref, convb_ref, dtb_ref, alog_ref,
                dskip_ref, nw_ref, o_ref, ext_ref, state_ref):
    ext_ref[CHUNK:2 * CHUNK, :] = xbc_ref[...]
    delayed = jnp.dot(shift_ref[...], ext_ref[...], preferred_element_type=jnp.float32)
    ext_ref[0:CHUNK, :] = xbc_ref[...]
    yield

    dt = jax.nn.softplus(dt_ref[...] + dtb_ref[...])
    a_log2 = -jnp.exp(alog_ref[...]) * LOG2E
    a_cum = dt * a_log2
    row = lax.broadcasted_iota(jnp.int32, (CHUNK, DT_PAD), 0)
    shift = 1
    while shift < CHUNK:
        a_cum = a_cum + jnp.where(row >= shift, pltpu.roll(a_cum, shift, 0), 0.0)
        shift *= 2
    a_cum_t = a_cum.T
    a_last = a_cum[CHUNK - 1:CHUNK, :]
    decay_to_end = jnp.exp2(a_last - a_cum)
    yield

    conv = convb_ref[...]
    for s in range(CONV_WIDTH):
        tap = CONV_WIDTH - 1 - s
        conv = conv + convw_ref[tap:tap + 1, :] * delayed[s * CHUNK:(s + 1) * CHUNK, :]
    xbc = _silu(conv)
    xs = xbc[:, :D_SSM]
    b_all = xbc[:, D_SSM:D_SSM + D_BC]
    c_all = xbc[:, D_SSM + D_BC:]
    yield

    xdt = xs * _expand_heads(dt)
    xdt_end = xs * _expand_heads(dt * decay_to_end)
    off_scale = _expand_heads(jnp.exp2(a_cum))
    state_scale = _expand_heads(jnp.exp2(a_last))
    gw = HEADS_PER_GROUP * SSM_HEAD_DIM
    cbs, y_offs = [], []
    for g in range(SSM_GROUPS):
        b_g = b_all[:, g * D_STATE:(g + 1) * D_STATE]
        c_g = c_all[:, g * D_STATE:(g + 1) * D_STATE]
        cbs.append(_bf16_dot_nt(c_g, b_g))
        state = state_ref[g]
        y_offs.append(_bf16_dot(c_g, state) * off_scale[:, g * gw:(g + 1) * gw])
        new_states = _bf16_dot(b_g.T, xdt_end[:, g * gw:(g + 1) * gw])
        state_ref[g] = state * state_scale[:, g * gw:(g + 1) * gw] + new_states
    yield

    t_idx = lax.broadcasted_iota(jnp.int32, (CHUNK, CHUNK), 0)
    s_idx = lax.broadcasted_iota(jnp.int32, (CHUNK, CHUNK), 1)
    causal = t_idx >= s_idx
    low_half = s_idx < SSM_HEAD_DIM
    pairs_per_group = HEADS_PER_GROUP // HEAD_PAIR
    y_tiles = []
    for g in range(SSM_GROUPS):
        for pp in range(pairs_per_group):
            pair = g * pairs_per_group + pp
            weights = []
            for u in range(HEAD_PAIR):
                h = pair * HEAD_PAIR + u
                seg = a_cum[:, h:h + 1] - a_cum_t[h:h + 1, :]
                weights.append(cbs[g] * jnp.exp2(jnp.where(causal, seg, -jnp.inf)))
            rhs = _split_halves(xdt[:, pair * LANES:(pair + 1) * LANES], low_half)
            y_diag = _bf16_dot(jnp.concatenate(weights, axis=1), rhs)
            y_tiles.append(y_diag + y_offs[g][:, pp * LANES:(pp + 1) * LANES])
        yield

    y = jnp.concatenate(y_tiles, axis=1) + dskip_ref[...] * xs
    y = y * _silu(z_ref[...].astype(jnp.float32))
    for g in range(SSM_GROUPS):
        yg = y[:, g * gw:(g + 1) * gw]
        o_ref[:, g * gw:(g + 1) * gw] = _rms_norm(
            yg, nw_ref[:, g * gw:(g + 1) * gw]).astype(o_ref.dtype)


def _attention_stages(c, sinks_ref, q_ref, k_ref, v_ref, eye_ref, bias_ref, o_ref,
                      kext_ref, vext_ref):
    kext_ref[WINDOW:2 * WINDOW, :] = k_ref[...]
    vext_ref[WINDOW:2 * WINDOW, :] = v_ref[...]
    k_ext = kext_ref[...]
    v_ext = vext_ref[...]
    kext_ref[0:WINDOW, :] = k_ref[...]
    vext_ref[0:WINDOW, :] = v_ref[...]
    bias_t = bias_ref[jnp.minimum(c, 1)]
    eye = eye_ref[...]
    low_kv = lax.broadcasted_iota(jnp.int32, (2 * WINDOW, LANES), 1) < HEAD_DIM
    low_half = lax.broadcasted_iota(jnp.int32, (CHUNK, LANES), 1) < HEAD_DIM
    k_swap = pltpu.roll(k_ext, HEAD_DIM, 1)
    v_swap = pltpu.roll(v_ext, HEAD_DIM, 1)
    zero_kv = jnp.zeros_like(k_ext)
    pairs_per_kv = Q_PER_KV // HEAD_PAIR
    rhs_v, scores = [], []
    for j in range(KV_HEADS):
        k_lo = jnp.where(low_kv, k_ext if j == 0 else k_swap, zero_kv)
        k_hi = jnp.where(low_kv, zero_kv, k_swap if j == 0 else k_ext)
        v_lo = jnp.where(low_kv, v_ext if j == 0 else v_swap, zero_kv)
        v_hi = jnp.where(low_kv, zero_kv, v_swap if j == 0 else v_ext)
        rhs_k = jnp.concatenate([jnp.concatenate([k_lo, bias_t], axis=1),
                                 jnp.concatenate([k_hi, bias_t], axis=1)], axis=0)
        rhs_v.append(jnp.concatenate([v_lo, v_hi], axis=0))
        for pp in range(pairs_per_kv):
            pair = j * pairs_per_kv + pp
            q_aug = jnp.concatenate([q_ref[:, pair * LANES:(pair + 1) * LANES], eye], axis=1)
            scores.append(_bf16_dot_nt(q_aug, rhs_k))
    yield

    probs, denoms = [], []
    for pair in range(ATTN_HEADS // HEAD_PAIR):
        pair_probs, pair_denoms = [], []
        for u in range(HEAD_PAIR):
            s_h = scores[pair][:, u * 2 * WINDOW:(u + 1) * 2 * WINDOW]
            sink = sinks_ref[pair * HEAD_PAIR + u] * LOG2E
            m = jnp.maximum(jnp.max(s_h, axis=-1, keepdims=True), sink)
            p = jnp.exp2(s_h - m)
            pair_probs.append(p.astype(jnp.bfloat16))
            pair_denoms.append(jnp.sum(p, axis=-1, keepdims=True) + jnp.exp2(sink - m))
        probs.append(jnp.concatenate(pair_probs, axis=1))
        denoms.append(jnp.where(low_half, pair_denoms[0], pair_denoms[1]))
        yield

    for pair in range(ATTN_HEADS // HEAD_PAIR):
        out = _bf16_dot(probs[pair], rhs_v[pair // pairs_per_kv])
        lo = D_SSM + pair * LANES
        o_ref[:, lo:lo + LANES] = (out / denoms[pair]).astype(o_ref.dtype)


def _mixer_tables():
    r = jnp.arange(CONV_WIDTH * CHUNK)[:, None]
    col = jnp.arange(2 * CHUNK)[None, :]
    shift = (col == CHUNK + r % CHUNK - r // CHUNK).astype(jnp.bfloat16)
    eye = jnp.eye(WINDOW, dtype=jnp.bfloat16)
    key = jnp.arange(2 * WINDOW)[:, None]
    qry = jnp.arange(WINDOW)[None, :] + WINDOW
    band = (key <= qry) & (key > qry - WINDOW)
    first = band & (key >= WINDOW)
    bias = jnp.stack([jnp.where(first, 0.0, MASKED), jnp.where(band, 0.0, MASKED)])
    return shift, eye, bias.astype(jnp.bfloat16)


def _mixer(z, xbc, dt, q, k, v, conv_w, conv_b, dt_bias, a_log, d_skip, ssm_norm_w,
           sinks, bsz, seq):
    nc = seq // CHUNK
    z, xbc, dt, q, k, v = (t.reshape(bsz, seq, t.shape[-1]) for t in (z, xbc, dt, q, k, v))
    shift, eye, bias = _mixer_tables()
    chunked = lambda width: pl.BlockSpec((ROWS_PER_STEP, CHUNK, width), lambda b, c: (b, c, 0))
    const = lambda shape: pl.BlockSpec(shape, lambda b, c: (0,) * len(shape))
    gw = HEADS_PER_GROUP * SSM_HEAD_DIM
    return pl.pallas_call(
        _mixer_kernel,
        grid=(bsz // ROWS_PER_STEP, nc),
        in_specs=[
            pl.BlockSpec(memory_space=pltpu.SMEM),
            chunked(D_SSM), chunked(D_CONV), chunked(DT_PAD),
            chunked(D_ATTN), chunked(D_KV), chunked(D_KV),
            const(shift.shape), const(eye.shape), const(bias.shape),
            const((CONV_WIDTH, D_CONV)), const((1, D_CONV)), const((1, DT_PAD)),
            const((1, DT_PAD)), const((1, D_SSM)), const((1, D_SSM)),
        ],
        out_specs=chunked(D_MIX),
        out_shape=jax.ShapeDtypeStruct((bsz, seq, D_MIX), jnp.bfloat16),
        scratch_shapes=[
            pltpu.VMEM((ROWS_PER_STEP, 2 * CHUNK, D_CONV), jnp.bfloat16),
            pltpu.VMEM((ROWS_PER_STEP, 2 * WINDOW, D_KV), jnp.bfloat16),
            pltpu.VMEM((ROWS_PER_STEP, 2 * WINDOW, D_KV), jnp.bfloat16),
            pltpu.VMEM((ROWS_PER_STEP, SSM_GROUPS, D_STATE, gw), jnp.float32),
        ],
        compiler_params=pltpu.CompilerParams(
            dimension_semantics=("arbitrary", "arbitrary"), vmem_limit_bytes=VMEM_LIMIT),
        name="mixer",
    )(sinks, z, xbc, dt, q, k, v, shift, eye, bias,
      conv_w, conv_b, dt_bias, a_log, d_skip, ssm_norm_w)


def _out_ffn_kernel(x_ref, y_ref, wo_ref, nffn_ref, wg_ref, wu_ref, wd_ref, nfin_ref, o_ref):
    x1 = x_ref[...] + jnp.dot(y_ref[...], wo_ref[...], preferred_element_type=jnp.float32)
    h = _rms_norm(x1, nffn_ref[...]).astype(jnp.bfloat16)
    gate = jnp.dot(h, wg_ref[...], preferred_element_type=jnp.float32)
    up = jnp.dot(h, wu_ref[...], preferred_element_type=jnp.float32)
    act = (_silu(gate) * up).astype(jnp.bfloat16)
    x2 = x1 + jnp.dot(act, wd_ref[...], preferred_element_type=jnp.float32)
    o_ref[...] = _rms_norm(x2, nfin_ref[...])


def _out_ffn(x2d, y2d, w_out, norm_ffn_w, w_gate, w_up, w_down, norm_final_w):
    n_tok = x2d.shape[0]
    tm = TOKEN_TILE
    row = lambda i: (i, 0)
    resident = lambda shape: pl.BlockSpec(shape, lambda i: (0, 0), pipeline_mode=pl.Buffered(1))
    return pl.pallas_call(
        _out_ffn_kernel,
        grid=(n_tok // tm,),
        in_specs=[
            pl.BlockSpec((tm, D_MODEL), row),
            pl.BlockSpec((tm, D_MIX), row),
            resident((D_MIX, D_MODEL)),
            resident((1, D_MODEL)),
            resident((D_MODEL, D_FF)),
            resident((D_MODEL, D_FF)),
            resident((D_FF, D_MODEL)),
            resident((1, D_MODEL)),
        ],
        out_specs=pl.BlockSpec((tm, D_MODEL), row),
        out_shape=jax.ShapeDtypeStruct((n_tok, D_MODEL), jnp.float32),
        compiler_params=pltpu.CompilerParams(
            dimension_semantics=("arbitrary",), vmem_limit_bytes=VMEM_LIMIT),
        name="out_ffn",
    )(x2d, y2d, w_out, norm_ffn_w, w_gate, w_up, w_down, norm_final_w)


def _pad_lanes(v, width):
    return jnp.pad(v, (0, width - v.shape[0])).reshape(1, width)


def _layer(x, norm_mix_w, w_in, conv_w, conv_b, dt_bias, a_log, d_skip, ssm_norm_w,
           attn_sinks, w_out, norm_ffn_w, w_gate, w_up, w_down, norm_out_w):
    bsz, seq, _ = x.shape
    bf16 = jnp.bfloat16
    s_xbc = D_SSM + D_CONV
    s_dt = s_xbc + SSM_HEADS
    s_q = s_dt + D_ATTN
    w_all = jnp.concatenate(
        [w_in[:, :s_xbc], w_in[:, s_dt:s_q] * (HEAD_DIM ** -0.5 * LOG2E), w_in[:, s_q:],
         w_in[:, s_xbc:s_dt], jnp.zeros((D_MODEL, DT_PAD - SSM_HEADS), w_in.dtype)],
        axis=1).astype(bf16)
    x2d = x.reshape(bsz * seq, D_MODEL)
    z, xbc, q, k, v, dt = _in_proj(x2d, norm_mix_w.reshape(1, D_MODEL), w_all)
    y = _mixer(z, xbc, dt, q, k, v, conv_w, conv_b.reshape(1, D_CONV),
               _pad_lanes(dt_bias, DT_PAD), _pad_lanes(a_log, DT_PAD),
               jnp.repeat(d_skip, SSM_HEAD_DIM).reshape(1, D_SSM),
               ssm_norm_w.reshape(1, D_SSM), attn_sinks, bsz, seq)
    out = _out_ffn(x2d, y.reshape(bsz * seq, D_MIX), w_out.astype(bf16),
                   norm_ffn_w.reshape(1, D_MODEL), w_gate.astype(bf16), w_up.astype(bf16),
                   w_down.astype(bf16), norm_out_w.reshape(1, D_MODEL))
    return out.reshape(bsz, seq, D_MODEL)


def kernel(x, norm_mix_w, w_in, conv_w, conv_b, dt_bias, a_log, d_skip, ssm_norm_w,
           attn_sinks, w_out, norm_ffn_w, w_gate, w_up, w_down, norm_final_w):
    depth = norm_mix_w.shape[0]
    assert depth == 1, "final RMSNorm is fused into the single layer's FFN kernel"
    return _layer(x, norm_mix_w[0], w_in[0], conv_w[0], conv_b[0], dt_bias[0], a_log[0],
                  d_skip[0], ssm_norm_w[0], attn_sinks[0], w_out[0], norm_ffn_w[0],
                  w_gate[0], w_up[0], w_down[0], norm_final_w)
```

```python
import math

import jax
import jax.numpy as jnp
from jax import lax
from jax.experimental import pallas as pl
from jax.experimental.pallas import tpu as pltpu

D_MODEL = 1024
SSM_HEADS = 8
SSM_HEAD_DIM = 64
D_SSM = SSM_HEADS * SSM_HEAD_DIM
SSM_GROUPS = 2
HEADS_PER_GROUP = SSM_HEADS // SSM_GROUPS
D_STATE = 128
CONV_WIDTH = 4
CHUNK = 128
D_BC = SSM_GROUPS * D_STATE
D_CONV = D_SSM + 2 * D_BC
ATTN_HEADS = 8
KV_HEADS = 2
Q_PER_KV = ATTN_HEADS // KV_HEADS
HEAD_DIM = 64
D_ATTN = ATTN_HEADS * HEAD_DIM
D_KV = KV_HEADS * HEAD_DIM
WINDOW = 128
D_MIX = D_SSM + D_ATTN
D_FF = 2816
EPS = 1e-5

LANES = 128
HEAD_PAIR = LANES // HEAD_DIM
DT_REP = LANES // SSM_HEADS
LOG2E = math.log2(math.e)
MASKED = -1e30
COL_Z = 0
COL_XBC = COL_Z + D_SSM
COL_K = COL_XBC + D_CONV
COL_DT = COL_K + D_KV
D_PROJ = COL_DT + LANES
D_PROJ_T = D_ATTN + D_KV

TOKEN_TILE = 512
ROWS_PER_STEP = 2
VMEM_LIMIT = 56 * 1024 * 1024

assert CHUNK == WINDOW == D_STATE == LANES and HEAD_DIM == SSM_HEAD_DIM and HEAD_PAIR == 2


def _rms_norm(x, w):
    return x * lax.rsqrt(jnp.mean(x * x, axis=-1, keepdims=True) + EPS) * w


def _silu(x):
    return x * jax.nn.sigmoid(x)


def _bf16_dot(a, b):
    return jnp.dot(a.astype(jnp.bfloat16), b.astype(jnp.bfloat16),
                   preferred_element_type=jnp.float32)


def _bf16_dot_nt(a, b):
    return lax.dot_general(a.astype(jnp.bfloat16), b.astype(jnp.bfloat16),
                           (((1,), (1,)), ((), ())),
                           preferred_element_type=jnp.float32)


def _in_proj_kernel(x_ref, nw_ref, w_ref, wt_ref, z_ref, xbc_ref, k_ref, dt_ref, qt_ref, vt_ref):
    h = _rms_norm(x_ref[...], nw_ref[...]).astype(jnp.bfloat16)
    p = jnp.dot(h, w_ref[...], preferred_element_type=jnp.float32)
    z_ref[...] = p[:, COL_Z:COL_XBC].astype(z_ref.dtype)
    xbc_ref[...] = p[:, COL_XBC:COL_K].astype(xbc_ref.dtype)
    k_ref[...] = p[:, COL_K:COL_DT].astype(k_ref.dtype)
    dt_ref[...] = p[:, COL_DT:D_PROJ]
    pt = _bf16_dot_nt(wt_ref[...], h)
    qt_ref[...] = pt[:D_ATTN, :].astype(qt_ref.dtype)
    vt_ref[...] = pt[D_ATTN:, :].astype(vt_ref.dtype)


def _in_proj(x2d, norm_w, w_all, w_t, bsz, seq):
    n_tok = x2d.shape[0]
    tm = TOKEN_TILE
    tiles_per_seq = seq // tm
    row = lambda i: (i, 0)
    const = lambda i: (0, 0)
    feat_major = lambda i: (i // tiles_per_seq, 0, i % tiles_per_seq)
    bf16 = jnp.bfloat16
    out_shapes = (
        jax.ShapeDtypeStruct((n_tok, D_SSM), bf16),
        jax.ShapeDtypeStruct((n_tok, D_CONV), bf16),
        jax.ShapeDtypeStruct((n_tok, D_KV), bf16),
        jax.ShapeDtypeStruct((n_tok, LANES), jnp.float32),
        jax.ShapeDtypeStruct((bsz, D_ATTN, seq), bf16),
        jax.ShapeDtypeStruct((bsz, D_KV, seq), bf16),
    )
    out_specs = (
        pl.BlockSpec((tm, D_SSM), row), pl.BlockSpec((tm, D_CONV), row),
        pl.BlockSpec((tm, D_KV), row), pl.BlockSpec((tm, LANES), row),
        pl.BlockSpec((None, D_ATTN, tm), feat_major), pl.BlockSpec((None, D_KV, tm), feat_major),
    )
    return pl.pallas_call(
        _in_proj_kernel,
        grid=(n_tok // tm,),
        in_specs=[
            pl.BlockSpec((tm, D_MODEL), row),
            pl.BlockSpec((1, D_MODEL), const),
            pl.BlockSpec((D_MODEL, D_PROJ), const),
            pl.BlockSpec((D_PROJ_T, D_MODEL), const),
        ],
        out_specs=out_specs,
        out_shape=out_shapes,
        compiler_params=pltpu.CompilerParams(
            dimension_semantics=("arbitrary",), vmem_limit_bytes=VMEM_LIMIT),
        name="in_proj",
    )(x2d, norm_w, w_all, w_t)


def _split_halves(x, low_half):
    zero = jnp.zeros_like(x)
    return jnp.concatenate([jnp.where(low_half, x, zero), jnp.where(low_half, zero, x)], axis=0)


def _hi_lo(x):
    hi = x.astype(jnp.bfloat16).astype(jnp.float32)
    return hi, x - hi


def _run_interleaved(stage_generators):
    pending = list(stage_generators)
    while pending:
        for gen in list(pending):
            try:
                next(gen)
            except StopIteration:
                pending.remove(gen)


def _mixer_kernel(sinks_ref, z_ref, xbc_ref, dt_ref, k_ref, qt_ref, vt_ref,
                  shift_ref, eye_ref, bias_ref, expand_ref,
                  convw_ref, convb_ref, dtb_ref, alog_ref, dskip_ref, nw_ref,
                  o_ref, ext_ref, kext_ref, vtext_ref, state_ref):
    c = pl.program_id(1)

    @pl.when(c == 0)
    def _():
        ext_ref[:, 0:CHUNK, :] = jnp.zeros((ROWS_PER_STEP, CHUNK, D_CONV), ext_ref.dtype)
        kext_ref[:, 0:WINDOW, :] = jnp.zeros((ROWS_PER_STEP, WINDOW, D_KV), kext_ref.dtype)
        vtext_ref[:, :, 0:WINDOW] = jnp.zeros((ROWS_PER_STEP, D_KV, WINDOW), vtext_ref.dtype)
        state_ref[...] = jnp.zeros_like(state_ref)

    stages = []
    for r in range(ROWS_PER_STEP):
        stages.append(_attention_stages(
            c, sinks_ref, k_ref.at[r], qt_ref.at[r], vt_ref.at[r], eye_ref, bias_ref,
            o_ref.at[r], kext_ref.at[r], vtext_ref.at[r]))
        stages.append(_ssd_stages(
            z_ref.at[r], xbc_ref.at[r], dt_ref.at[r], shift_ref, expand_ref, convw_ref,
            convb_ref, dtb_ref, alog_ref, dskip_ref, nw_ref, o_ref.at[r], ext_ref.at[r],
            state_ref.at[r]))
    _run_interleaved(stages)


def _ssd_stages(z_ref, xbc_ref, dt_ref, shift_ref, expand_ref, convw_ref, convb_ref, dtb_ref,
                alog_ref, dskip_ref, nw_ref, o_ref, ext_ref, state_ref):
    ext_ref[CHUNK:2 * CHUNK, :] = xbc_ref[...]
    delayed = jnp.dot(shift_ref[...], ext_ref[...], preferred_element_type=jnp.float32)
    ext_ref[0:CHUNK, :] = xbc_ref[...]
    yield

    dt = jax.nn.softplus(dt_ref[...] + dtb_ref[...])
    a_log2 = -jnp.exp(alog_ref[...]) * LOG2E
    a_cum = dt * a_log2
    row = lax.broadcasted_iota(jnp.int32, (CHUNK, LANES), 0)
    shift = 1
    while shift < CHUNK:
        a_cum = a_cum + jnp.where(row >= shift, pltpu.roll(a_cum, shift, 0), 0.0)
        shift *= 2
    a_cum_t = a_cum.T
    a_last = a_cum[CHUNK - 1:CHUNK, :]
    sub = lax.broadcasted_iota(jnp.int32, (CHUNK, LANES), 1) % DT_REP
    packed = jnp.zeros((CHUNK, LANES), jnp.float32)
    for i, val in enumerate((dt, dt * jnp.exp2(a_last - a_cum), jnp.exp2(a_cum))):
        for j, part in enumerate(_hi_lo(val)):
            packed = jnp.where(sub == 2 * i + j, part, packed)
    spread = _bf16_dot(packed, expand_ref[...])
    dt_x = spread[:, :D_SSM]
    dt_end_x = spread[:, D_SSM:2 * D_SSM]
    off_scale = spread[:, 2 * D_SSM:]
    state_scale = off_scale[CHUNK - 1:CHUNK, :]
    yield

    conv = convb_ref[...]
    for s in range(CONV_WIDTH):
        tap = CONV_WIDTH - 1 - s
        conv = conv + convw_ref[tap:tap + 1, :] * delayed[s * CHUNK:(s + 1) * CHUNK, :]
    xbc = _silu(conv)
    xs = xbc[:, :D_SSM]
    b_all = xbc[:, D_SSM:D_SSM + D_BC]
    c_all = xbc[:, D_SSM + D_BC:]
    yield

    xdt = xs * dt_x
    xdt_end = xs * dt_end_x
    gw = HEADS_PER_GROUP * SSM_HEAD_DIM
    cbs, y_offs = [], []
    for g in range(SSM_GROUPS):
        b_g = b_all[:, g * D_STATE:(g + 1) * D_STATE]
        c_g = c_all[:, g * D_STATE:(g + 1) * D_STATE]
        cbs.append(_bf16_dot_nt(c_g, b_g))
        state = state_ref[g]
        y_offs.append(_bf16_dot(c_g, state) * off_scale[:, g * gw:(g + 1) * gw])
        new_states = _bf16_dot(b_g.T, xdt_end[:, g * gw:(g + 1) * gw])
        state_ref[g] = state * state_scale[:, g * gw:(g + 1) * gw] + new_states
    yield

    t_idx = lax.broadcasted_iota(jnp.int32, (CHUNK, CHUNK), 0)
    s_idx = lax.broadcasted_iota(jnp.int32, (CHUNK, CHUNK), 1)
    causal = t_idx >= s_idx
    low_half = s_idx < SSM_HEAD_DIM
    pairs_per_group = HEADS_PER_GROUP // HEAD_PAIR
    y_tiles = []
    for g in range(SSM_GROUPS):
        for pp in range(pairs_per_group):
            pair = g * pairs_per_group + pp
            weights = []
            for u in range(HEAD_PAIR):
                col = (pair * HEAD_PAIR + u) * DT_REP
                seg = a_cum[:, col:col + 1] - a_cum_t[col:col + 1, :]
                weights.append(cbs[g] * jnp.exp2(jnp.where(causal, seg, -jnp.inf)))
            rhs = _split_halves(xdt[:, pair * LANES:(pair + 1) * LANES], low_half)
            y_diag = _bf16_dot(jnp.concatenate(weights, axis=1), rhs)
            y_tiles.append(y_diag + y_offs[g][:, pp * LANES:(pp + 1) * LANES])
        yield

    y = jnp.concatenate(y_tiles, axis=1) + dskip_ref[...] * xs
    y = y * _silu(z_ref[...].astype(jnp.float32))
    for g in range(SSM_GROUPS):
        yg = y[:, g * gw:(g + 1) * gw]
        o_ref[:, g * gw:(g + 1) * gw] = _rms_norm(
            yg, nw_ref[:, g * gw:(g + 1) * gw]).astype(o_ref.dtype)


def _attention_stages(c, sinks_ref, k_ref, qt_ref, vt_ref, eye_ref, bias_ref, o_ref,
                      kext_ref, vtext_ref):
    kext_ref[WINDOW:2 * WINDOW, :] = k_ref[...]
    vtext_ref[:, WINDOW:2 * WINDOW] = vt_ref[...]
    k_ext = kext_ref[...]
    vt_ext = vtext_ref[...]
    kext_ref[0:WINDOW, :] = k_ref[...]
    vtext_ref[:, 0:WINDOW] = vt_ref[...]
    bias_t = bias_ref[jnp.minimum(c, 1)]
    eye2 = eye_ref[...]
    low_kv = lax.broadcasted_iota(jnp.int32, (2 * WINDOW, LANES), 1) < HEAD_DIM
    k_swap = pltpu.roll(k_ext, HEAD_DIM, 1)
    zero_k = jnp.zeros_like(k_ext)
    zero_vt = jnp.zeros((HEAD_DIM, 2 * WINDOW), vt_ext.dtype)
    pairs_per_kv = Q_PER_KV // HEAD_PAIR
    assert pairs_per_kv == 2
    logits, values_t = [], []
    for j in range(KV_HEADS):
        k_lo = jnp.where(low_kv, k_ext if j == 0 else k_swap, zero_k)
        k_hi = jnp.where(low_kv, zero_k, k_swap if j == 0 else k_ext)
        keys = jnp.concatenate([jnp.concatenate([k_lo, bias_t], axis=1),
                                jnp.concatenate([k_hi, bias_t], axis=1)], axis=0)
        rows = slice(j * pairs_per_kv * LANES, (j + 1) * pairs_per_kv * LANES)
        q_t = qt_ref[rows, :]
        queries = jnp.concatenate(
            [jnp.concatenate([q_t[:LANES, :], q_t[LANES:, :]], axis=1), eye2], axis=0)
        logits.append(jnp.dot(keys, queries, preferred_element_type=jnp.float32))
        v_t = vt_ext[j * HEAD_DIM:(j + 1) * HEAD_DIM, :]
        values_t.append(jnp.concatenate([jnp.concatenate([v_t, zero_vt], axis=1),
                                         jnp.concatenate([zero_vt, v_t], axis=1)], axis=0))
    yield

    top_rows = lax.broadcasted_iota(jnp.int32, (LANES, LANES), 0) < HEAD_DIM
    for j in range(KV_HEADS):
        prob_rows, denom_tiles = [], []
        for u in range(HEAD_PAIR):
            prob_tiles, denoms = [], []
            for pp in range(pairs_per_kv):
                s_h = logits[j][u * 2 * WINDOW:(u + 1) * 2 * WINDOW, pp * LANES:(pp + 1) * LANES]
                sink = sinks_ref[(j * pairs_per_kv + pp) * HEAD_PAIR + u] * LOG2E
                m = jnp.maximum(jnp.max(s_h, axis=0, keepdims=True), sink)
                p = jnp.exp2(s_h - m)
                prob_tiles.append(p.astype(jnp.bfloat16))
                denoms.append(jnp.sum(p, axis=0, keepdims=True) + jnp.exp2(sink - m))
            prob_rows.append(jnp.concatenate(prob_tiles, axis=1))
            denom_tiles.append(denoms)
            yield
        probs_t = jnp.concatenate(prob_rows, axis=0)
        out_t = jnp.dot(values_t[j], probs_t, preferred_element_type=jnp.float32)
        for pp in range(pairs_per_kv):
            denom = jnp.where(top_rows, denom_tiles[0][pp], denom_tiles[1][pp])
            out = (out_t[:, pp * LANES:(pp + 1) * LANES] / denom).T
            lo = D_SSM + (j * pairs_per_kv + pp) * LANES
            o_ref[:, lo:lo + LANES] = out.astype(o_ref.dtype)
        yield


def _mixer_tables():
    bf16 = jnp.bfloat16
    r = jnp.arange(CONV_WIDTH * CHUNK)[:, None]
    col = jnp.arange(2 * CHUNK)[None, :]
    shift = (col == CHUNK + r % CHUNK - r // CHUNK).astype(bf16)
    eye = jnp.eye(WINDOW, dtype=bf16)
    eye2 = jnp.concatenate([eye, eye], axis=1)
    key = jnp.arange(2 * WINDOW)[:, None]
    qry = jnp.arange(WINDOW)[None, :] + WINDOW
    band = (key <= qry) & (key > qry - WINDOW)
    first = band & (key >= WINDOW)
    bias = jnp.stack([jnp.where(first, 0.0, MASKED), jnp.where(band, 0.0, MASKED)]).astype(bf16)
    lane = jnp.arange(LANES)[:, None]
    out = jnp.arange(3 * D_SSM)[None, :]
    quantity = (lane % DT_REP) // 2
    expand = ((out // D_SSM == quantity) & ((out % D_SSM) // SSM_HEAD_DIM == lane // DT_REP)
              & (lane % DT_REP < 6)).astype(bf16)
    return shift, eye2, bias, expand


def _mixer(z, xbc, dt, k, qt, vt, conv_w, conv_b, dt_bias, a_log, d_skip, ssm_norm_w,
           sinks, bsz, seq):
    nc = seq // CHUNK
    z, xbc, dt, k = (t.reshape(bsz, seq, t.shape[-1]) for t in (z, xbc, dt, k))
    shift, eye2, bias, expand = _mixer_tables()
    chunked = lambda width: pl.BlockSpec((ROWS_PER_STEP, CHUNK, width), lambda b, c: (b, c, 0))
    chunked_t = lambda feats: pl.BlockSpec((ROWS_PER_STEP, feats, CHUNK), lambda b, c: (b, 0, c))
    const = lambda shape: pl.BlockSpec(shape, lambda b, c: (0,) * len(shape))
    gw = HEADS_PER_GROUP * SSM_HEAD_DIM
    return pl.pallas_call(
        _mixer_kernel,
        grid=(bsz // ROWS_PER_STEP, nc),
        in_specs=[
            pl.BlockSpec(memory_space=pltpu.SMEM),
            chunked(D_SSM), chunked(D_CONV), chunked(LANES), chunked(D_KV),
            chunked_t(D_ATTN), chunked_t(D_KV),
            const(shift.shape), const(eye2.shape), const(bias.shape), const(expand.shape),
            const((CONV_WIDTH, D_CONV)), const((1, D_CONV)), const((1, LANES)),
            const((1, LANES)), const((1, D_SSM)), const((1, D_SSM)),
        ],
        out_specs=chunked(D_MIX),
        out_shape=jax.ShapeDtypeStruct((bsz, seq, D_MIX), jnp.bfloat16),
        scratch_shapes=[
            pltpu.VMEM((ROWS_PER_STEP, 2 * CHUNK, D_CONV), jnp.bfloat16),
            pltpu.VMEM((ROWS_PER_STEP, 2 * WINDOW, D_KV), jnp.bfloat16),
            pltpu.VMEM((ROWS_PER_STEP, D_KV, 2 * WINDOW), jnp.bfloat16),
            pltpu.VMEM((ROWS_PER_STEP, SSM_GROUPS, D_STATE, gw), jnp.float32),
        ],
        compiler_params=pltpu.CompilerParams(
            dimension_semantics=("arbitrary", "arbitrary"), vmem_limit_bytes=VMEM_LIMIT),
        name="mixer",
    )(sinks, z, xbc, dt, k, qt, vt, shift, eye2, bias, expand,
      conv_w, conv_b, dt_bias, a_log, d_skip, ssm_norm_w)


def _out_ffn_kernel(x_ref, y_ref, wo_ref, nffn_ref, wg_ref, wu_ref, wd_ref, nfin_ref, o_ref):
    x1 = x_ref[...] + jnp.dot(y_ref[...], wo_ref[...], preferred_element_type=jnp.float32)
    h = _rms_norm(x1, nffn_ref[...]).astype(jnp.bfloat16)
    gate = jnp.dot(h, wg_ref[...], preferred_element_type=jnp.float32)
    up = jnp.dot(h, wu_ref[...], preferred_element_type=jnp.float32)
    act = (_silu(gate) * up).astype(jnp.bfloat16)
    x2 = x1 + jnp.dot(act, wd_ref[...], preferred_element_type=jnp.float32)
    o_ref[...] = _rms_norm(x2, nfin_ref[...])


def _out_ffn(x2d, y2d, w_out, norm_ffn_w, w_gate, w_up, w_down, norm_final_w):
    n_tok = x2d.shape[0]
    tm = TOKEN_TILE
    row = lambda i: (i, 0)
    resident = lambda shape: pl.BlockSpec(shape, lambda i: (0, 0), pipeline_mode=pl.Buffered(1))
    return pl.pallas_call(
        _out_ffn_kernel,
        grid=(n_tok // tm,),
        in_specs=[
            pl.BlockSpec((tm, D_MODEL), row),
            pl.BlockSpec((tm, D_MIX), row),
            resident((D_MIX, D_MODEL)),
            resident((1, D_MODEL)),
            resident((D_MODEL, D_FF)),
            resident((D_MODEL, D_FF)),
            resident((D_FF, D_MODEL)),
            resident((1, D_MODEL)),
        ],
        out_specs=pl.BlockSpec((tm, D_MODEL), row),
        out_shape=jax.ShapeDtypeStruct((n_tok, D_MODEL), jnp.float32),
        compiler_params=pltpu.CompilerParams(
            dimension_semantics=("arbitrary",), vmem_limit_bytes=VMEM_LIMIT),
        name="out_ffn",
    )(x2d, y2d, w_out, norm_ffn_w, w_gate, w_up, w_down, norm_final_w)


def _per_head_lanes(v):
    return jnp.repeat(v, DT_REP).reshape(1, LANES)


def _layer(x, norm_mix_w, w_in, conv_w, conv_b, dt_bias, a_log, d_skip, ssm_norm_w,
           attn_sinks, w_out, norm_ffn_w, w_gate, w_up, w_down, norm_out_w):
    bsz, seq, _ = x.shape
    bf16 = jnp.bfloat16
    s_xbc = D_SSM + D_CONV
    s_dt = s_xbc + SSM_HEADS
    s_q = s_dt + D_ATTN
    s_k = s_q + D_KV
    w_all = jnp.concatenate(
        [w_in[:, :s_xbc], w_in[:, s_q:s_k], jnp.repeat(w_in[:, s_xbc:s_dt], DT_REP, axis=1)],
        axis=1).astype(bf16)
    w_t = jnp.concatenate(
        [w_in[:, s_dt:s_q] * (HEAD_DIM ** -0.5 * LOG2E), w_in[:, s_k:]], axis=1).T.astype(bf16)
    x2d = x.reshape(bsz * seq, D_MODEL)
    z, xbc, k, dt, qt, vt = _in_proj(x2d, norm_mix_w.reshape(1, D_MODEL), w_all, w_t, bsz, seq)
    y = _mixer(z, xbc, dt, k, qt, vt, conv_w, conv_b.reshape(1, D_CONV),
               _per_head_lanes(dt_bias), _per_head_lanes(a_log),
               jnp.repeat(d_skip, SSM_HEAD_DIM).reshape(1, D_SSM),
               ssm_norm_w.reshape(1, D_SSM), attn_sinks, bsz, seq)
    out = _out_ffn(x2d, y.reshape(bsz * seq, D_MIX), w_out.astype(bf16),
                   norm_ffn_w.reshape(1, D_MODEL), w_gate.astype(bf16), w_up.astype(bf16),
                   w_down.astype(bf16), norm_out_w.reshape(1, D_MODEL))
    return out.reshape(bsz, seq, D_MODEL)


def kernel(x, norm_mix_w, w_in, conv_w, conv_b, dt_bias, a_log, d_skip, ssm_norm_w,
           attn_sinks, w_out, norm_ffn_w, w_gate, w_up, w_down, norm_final_w):
    depth = norm_mix_w.shape[0]
    assert depth == 1, "final RMSNorm is fused into the single layer's FFN kernel"
    return _layer(x, norm_mix_w[0], w_in[0], conv_w[0], conv_b[0], dt_bias[0], a_log[0],
                  d_skip[0], ssm_norm_w[0], attn_sinks[0], w_out[0], norm_ffn_w[0],
                  w_gate[0], w_up[0], w_down[0], norm_final_w)
```

```python
import functools
import math

import jax
import jax.numpy as jnp
from jax import lax
from jax.experimental import pallas as pl
from jax.experimental.pallas import tpu as pltpu

D_MODEL = 1024
SSM_HEADS = 8
SSM_HEAD_DIM = 64
D_SSM = SSM_HEADS * SSM_HEAD_DIM
SSM_GROUPS = 2
HEADS_PER_GROUP = SSM_HEADS // SSM_GROUPS
D_STATE = 128
CONV_WIDTH = 4
CHUNK = 128
D_BC = SSM_GROUPS * D_STATE
D_CONV = D_SSM + 2 * D_BC
ATTN_HEADS = 8
KV_HEADS = 2
Q_PER_KV = ATTN_HEADS // KV_HEADS
HEAD_DIM = 64
D_ATTN = ATTN_HEADS * HEAD_DIM
D_KV = KV_HEADS * HEAD_DIM
WINDOW = 128
D_MIX = D_SSM + D_ATTN
D_FF = 2816
EPS = 1e-5

LANES = 128
SUBLANES = 8
HEAD_PAIR = LANES // HEAD_DIM
DT_REP = LANES // SSM_HEADS
LOG2E = math.log2(math.e)
MASKED = -1e30
COL_Z = 0
COL_XBC = COL_Z + D_SSM
COL_K = COL_XBC + D_CONV
COL_DT = COL_K + D_KV
D_PROJ = COL_DT + LANES
D_PROJ_T = D_ATTN + D_KV

TOKEN_TILE = 512
ROWS_PER_STEP = 2
VMEM_LIMIT = 56 * 1024 * 1024

assert CHUNK == WINDOW == D_STATE == LANES and HEAD_DIM == SSM_HEAD_DIM and HEAD_PAIR == 2


def _rms_norm(x, w):
    return x * lax.rsqrt(jnp.mean(x * x, axis=-1, keepdims=True) + EPS) * w


def _silu(x):
    return x * jax.nn.sigmoid(x)


def _bf16_dot(a, b):
    return jnp.dot(a.astype(jnp.bfloat16), b.astype(jnp.bfloat16),
                   preferred_element_type=jnp.float32)


def _bf16_dot_nt(a, b):
    return lax.dot_general(a.astype(jnp.bfloat16), b.astype(jnp.bfloat16),
                           (((1,), (1,)), ((), ())),
                           preferred_element_type=jnp.float32)


def _in_proj_stages(x_ref, nw_ref, w_ref, wt_ref, z_ref, xbc_ref, k_ref, dt_ref, qt_ref, vt_ref):
    rows = ROWS_PER_STEP * CHUNK
    h = _rms_norm(x_ref[...].reshape(rows, D_MODEL), nw_ref[...]).astype(jnp.bfloat16)
    yield
    p = jnp.dot(h, w_ref[...], preferred_element_type=jnp.float32)
    for r in range(ROWS_PER_STEP):
        p_r = p[r * CHUNK:(r + 1) * CHUNK, :]
        z_ref[r] = p_r[:, COL_Z:COL_XBC].astype(z_ref.dtype)
        xbc_ref[r] = p_r[:, COL_XBC:COL_K].astype(xbc_ref.dtype)
        k_ref[r] = p_r[:, COL_K:COL_DT].astype(k_ref.dtype)
        dt_ref[r] = p_r[:, COL_DT:D_PROJ]
    yield
    pt = _bf16_dot_nt(wt_ref[...], h)
    for r in range(ROWS_PER_STEP):
        pt_r = pt[:, r * CHUNK:(r + 1) * CHUNK]
        qt_ref[r] = pt_r[:D_ATTN, :].astype(qt_ref.dtype)
        vt_ref[r] = pt_r[D_ATTN:, :].astype(vt_ref.dtype)


def _split_halves(x, low_half):
    zero = jnp.zeros_like(x)
    return jnp.concatenate([jnp.where(low_half, x, zero), jnp.where(low_half, zero, x)], axis=0)


def _hi_lo(x):
    hi = x.astype(jnp.bfloat16).astype(jnp.float32)
    return hi, x - hi


def _run_interleaved(stage_generators):
    pending = list(stage_generators)
    while pending:
        for gen in list(pending):
            try:
                next(gen)
            except StopIteration:
                pending.remove(gen)


def _proj_mixer_kernel(nc, sinks_ref, x_ref, nmix_ref, w_ref, wt_ref,
                       eye_ref, bias_ref, expand_ref,
                       convw_ref, convb_ref, dtb_ref, alog_ref, dskip_ref, nw_ref,
                       o_ref, z_buf, xbc_buf, k_buf, dt_buf, qt_buf, vt_buf,
                       tail_ref, kext_ref, vtext_ref, state_ref):
    s = pl.program_id(0)
    c = jnp.maximum(lax.rem(s - 1, nc), 0)

    @pl.when(s == 0)
    def _():
        for ref in (z_buf, xbc_buf, k_buf, dt_buf, qt_buf, vt_buf):
            ref[...] = jnp.zeros_like(ref)

    @pl.when(c == 0)
    def _():
        tail_ref[...] = jnp.zeros_like(tail_ref)
        kext_ref[:, 0:WINDOW, :] = jnp.zeros((ROWS_PER_STEP, WINDOW, D_KV), kext_ref.dtype)
        vtext_ref[:, :, 0:WINDOW] = jnp.zeros((ROWS_PER_STEP, D_KV, WINDOW), vtext_ref.dtype)
        state_ref[...] = jnp.zeros_like(state_ref)

    def step(wr):
        rd = 1 - wr
        stages = [_in_proj_stages(x_ref, nmix_ref, w_ref, wt_ref, z_buf.at[wr], xbc_buf.at[wr],
                                  k_buf.at[wr], dt_buf.at[wr], qt_buf.at[wr], vt_buf.at[wr])]
        for r in range(ROWS_PER_STEP):
            stages.append(_attention_stages(
                c, sinks_ref, k_buf.at[rd, r], qt_buf.at[rd, r], vt_buf.at[rd, r], eye_ref,
                bias_ref, o_ref.at[r], kext_ref.at[r], vtext_ref.at[r]))
            stages.append(_ssd_stages(
                z_buf.at[rd, r], xbc_buf.at[rd, r], dt_buf.at[rd, r], expand_ref, convw_ref,
                convb_ref, dtb_ref, alog_ref, dskip_ref, nw_ref, o_ref.at[r], tail_ref.at[r],
                state_ref.at[r]))
        _run_interleaved(stages)

    parity = lax.rem(s, 2)
    for wr in range(2):
        pl.when(parity == wr)(functools.partial(step, wr))


def _ssd_stages(z_ref, xbc_ref, dt_ref, expand_ref, convw_ref, convb_ref, dtb_ref,
                alog_ref, dskip_ref, nw_ref, o_ref, tail_ref, state_ref):
    cur = xbc_ref[...].astype(jnp.float32)
    tail = tail_ref[...]
    tail_ref[...] = cur[CHUNK - SUBLANES:, :]
    first_rows = lax.broadcasted_iota(jnp.int32, (SUBLANES, D_CONV), 0)
    delayed = [cur]
    for s in range(1, CONV_WIDTH):
        rolled = pltpu.roll(cur, s, 0)
        head = jnp.where(first_rows < s, pltpu.roll(tail, s, 0), rolled[:SUBLANES, :])
        delayed.append(jnp.concatenate([head, rolled[SUBLANES:, :]], axis=0))
    yield

    dt = jax.nn.softplus(dt_ref[...] + dtb_ref[...])
    a_log2 = -jnp.exp(alog_ref[...]) * LOG2E
    a_cum = dt * a_log2
    row = lax.broadcasted_iota(jnp.int32, (CHUNK, LANES), 0)
    shift = 1
    while shift < CHUNK:
        a_cum = a_cum + jnp.where(row >= shift, pltpu.roll(a_cum, shift, 0), 0.0)
        shift *= 2
    a_cum_t = a_cum.T
    a_last = a_cum[CHUNK - 1:CHUNK, :]
    sub = lax.broadcasted_iota(jnp.int32, (CHUNK, LANES), 1) % DT_REP
    packed = jnp.zeros((CHUNK, LANES), jnp.float32)
    for i, val in enumerate((dt, dt * jnp.exp2(a_last - a_cum), jnp.exp2(a_cum))):
        for j, part in enumerate(_hi_lo(val)):
            packed = jnp.where(sub == 2 * i + j, part, packed)
    spread = _bf16_dot(packed, expand_ref[...])
    dt_x = spread[:, :D_SSM]
    dt_end_x = spread[:, D_SSM:2 * D_SSM]
    off_scale = spread[:, 2 * D_SSM:]
    state_scale = off_scale[CHUNK - 1:CHUNK, :]
    yield

    conv = convb_ref[...]
    for s in range(CONV_WIDTH):
        tap = CONV_WIDTH - 1 - s
        conv = conv + convw_ref[tap:tap + 1, :] * delayed[s]
    xbc = _silu(conv)
    xs = xbc[:, :D_SSM]
    b_all = xbc[:, D_SSM:D_SSM + D_BC]
    c_all = xbc[:, D_SSM + D_BC:]
    yield

    xdt = xs * dt_x
    xdt_end = xs * dt_end_x
    gw = HEADS_PER_GROUP * SSM_HEAD_DIM
    cbs, y_offs = [], []
    for g in range(SSM_GROUPS):
        b_g = b_all[:, g * D_STATE:(g + 1) * D_STATE]
        c_g = c_all[:, g * D_STATE:(g + 1) * D_STATE]
        cbs.append(_bf16_dot_nt(c_g, b_g))
        state = state_ref[g]
        y_offs.append(_bf16_dot(c_g, state) * off_scale[:, g * gw:(g + 1) * gw])
        new_states = _bf16_dot(b_g.T, xdt_end[:, g * gw:(g + 1) * gw])
        state_ref[g] = state * state_scale[:, g * gw:(g + 1) * gw] + new_states
    yield

    t_idx = lax.broadcasted_iota(jnp.int32, (CHUNK, CHUNK), 0)
    s_idx = lax.broadcasted_iota(jnp.int32, (CHUNK, CHUNK), 1)
    causal = t_idx >= s_idx
    low_half = s_idx < SSM_HEAD_DIM
    pairs_per_group = HEADS_PER_GROUP // HEAD_PAIR
    y_tiles = []
    for g in range(SSM_GROUPS):
        for pp in range(pairs_per_group):
            pair = g * pairs_per_group + pp
            weights = []
            for u in range(HEAD_PAIR):
                col = (pair * HEAD_PAIR + u) * DT_REP
                seg = a_cum[:, col:col + 1] - a_cum_t[col:col + 1, :]
                weights.append(cbs[g] * jnp.exp2(jnp.where(causal, seg, -jnp.inf)))
            rhs = _split_halves(xdt[:, pair * LANES:(pair + 1) * LANES], low_half)
            y_diag = _bf16_dot(jnp.concatenate(weights, axis=1), rhs)
            y_tiles.append(y_diag + y_offs[g][:, pp * LANES:(pp + 1) * LANES])
        yield

    y = jnp.concatenate(y_tiles, axis=1) + dskip_ref[...] * xs
    y = y * _silu(z_ref[...].astype(jnp.float32))
    for g in range(SSM_GROUPS):
        yg = y[:, g * gw:(g + 1) * gw]
        o_ref[:, g * gw:(g + 1) * gw] = _rms_norm(
            yg, nw_ref[:, g * gw:(g + 1) * gw]).astype(o_ref.dtype)


def _attention_stages(c, sinks_ref, k_ref, qt_ref, vt_ref, eye_ref, bias_ref, o_ref,
                      kext_ref, vtext_ref):
    kext_ref[WINDOW:2 * WINDOW, :] = k_ref[...]
    vtext_ref[:, WINDOW:2 * WINDOW] = vt_ref[...]
    k_ext = kext_ref[...]
    vt_ext = vtext_ref[...]
    kext_ref[0:WINDOW, :] = k_ref[...]
    vtext_ref[:, 0:WINDOW] = vt_ref[...]
    bias_t = bias_ref[jnp.minimum(c, 1)]
    eye2 = eye_ref[...]
    low_kv = lax.broadcasted_iota(jnp.int32, (2 * WINDOW, LANES), 1) < HEAD_DIM
    k_swap = pltpu.roll(k_ext, HEAD_DIM, 1)
    zero_k = jnp.zeros_like(k_ext)
    zero_vt = jnp.zeros((HEAD_DIM, 2 * WINDOW), vt_ext.dtype)
    pairs_per_kv = Q_PER_KV // HEAD_PAIR
    assert pairs_per_kv == 2
    logits, values_t = [], []
    for j in range(KV_HEADS):
        k_lo = jnp.where(low_kv, k_ext if j == 0 else k_swap, zero_k)
        k_hi = jnp.where(low_kv, zero_k, k_swap if j == 0 else k_ext)
        keys = jnp.concatenate([jnp.concatenate([k_lo, bias_t], axis=1),
                                jnp.concatenate([k_hi, bias_t], axis=1)], axis=0)
        rows = slice(j * pairs_per_kv * LANES, (j + 1) * pairs_per_kv * LANES)
        q_t = qt_ref[rows, :]
        queries = jnp.concatenate(
            [jnp.concatenate([q_t[:LANES, :], q_t[LANES:, :]], axis=1), eye2], axis=0)
        logits.append(jnp.dot(keys, queries, preferred_element_type=jnp.float32))
        v_t = vt_ext[j * HEAD_DIM:(j + 1) * HEAD_DIM, :]
        values_t.append(jnp.concatenate([jnp.concatenate([v_t, zero_vt], axis=1),
                                         jnp.concatenate([zero_vt, v_t], axis=1)], axis=0))
    yield

    top_rows = lax.broadcasted_iota(jnp.int32, (LANES, LANES), 0) < HEAD_DIM
    for j in range(KV_HEADS):
        prob_rows, denom_tiles = [], []
        for u in range(HEAD_PAIR):
            prob_tiles, denoms = [], []
            for pp in range(pairs_per_kv):
                s_h = logits[j][u * 2 * WINDOW:(u + 1) * 2 * WINDOW, pp * LANES:(pp + 1) * LANES]
                sink = sinks_ref[(j * pairs_per_kv + pp) * HEAD_PAIR + u] * LOG2E
                m = jnp.maximum(jnp.max(s_h, axis=0, keepdims=True), sink)
                p = jnp.exp2(s_h - m)
                prob_tiles.append(p.astype(jnp.bfloat16))
                denoms.append(jnp.sum(p, axis=0, keepdims=True) + jnp.exp2(sink - m))
            prob_rows.append(jnp.concatenate(prob_tiles, axis=1))
            denom_tiles.append(denoms)
            yield
        probs_t = jnp.concatenate(prob_rows, axis=0)
        out_t = jnp.dot(values_t[j], probs_t, preferred_element_type=jnp.float32)
        for pp in range(pairs_per_kv):
            denom = jnp.where(top_rows, denom_tiles[0][pp], denom_tiles[1][pp])
            out = (out_t[:, pp * LANES:(pp + 1) * LANES] / denom).T
            lo = D_SSM + (j * pairs_per_kv + pp) * LANES
            o_ref[:, lo:lo + LANES] = out.astype(o_ref.dtype)
        yield


def _mixer_tables():
    bf16 = jnp.bfloat16
    eye = jnp.eye(WINDOW, dtype=bf16)
    eye2 = jnp.concatenate([eye, eye], axis=1)
    key = jnp.arange(2 * WINDOW)[:, None]
    qry = jnp.arange(WINDOW)[None, :] + WINDOW
    band = (key <= qry) & (key > qry - WINDOW)
    first = band & (key >= WINDOW)
    bias = jnp.stack([jnp.where(first, 0.0, MASKED), jnp.where(band, 0.0, MASKED)]).astype(bf16)
    lane = jnp.arange(LANES)[:, None]
    out = jnp.arange(3 * D_SSM)[None, :]
    quantity = (lane % DT_REP) // 2
    expand = ((out // D_SSM == quantity) & ((out % D_SSM) // SSM_HEAD_DIM == lane // DT_REP)
              & (lane % DT_REP < 6)).astype(bf16)
    return eye2, bias, expand


def _proj_mixer(x, norm_mix_w, w_all, w_t, conv_w, conv_b, dt_bias, a_log, d_skip, ssm_norm_w,
                sinks):
    bsz, seq, _ = x.shape
    nc = seq // CHUNK
    n_groups = (bsz // ROWS_PER_STEP) * nc
    eye2, bias, expand = _mixer_tables()
    group = lambda width, lag: pl.BlockSpec(
        (ROWS_PER_STEP, CHUNK, width),
        lambda s: (jnp.clip(s - lag, 0, n_groups - 1) // nc, jnp.clip(s - lag, 0, n_groups - 1) % nc, 0))
    const = lambda shape: pl.BlockSpec(shape, lambda s: (0,) * len(shape))
    gw = HEADS_PER_GROUP * SSM_HEAD_DIM
    bf16 = jnp.bfloat16
    handoff = lambda rows, cols, dtype: pltpu.VMEM((2, ROWS_PER_STEP, rows, cols), dtype)
    return pl.pallas_call(
        functools.partial(_proj_mixer_kernel, nc),
        grid=(n_groups + 1,),
        in_specs=[
            pl.BlockSpec(memory_space=pltpu.SMEM),
            group(D_MODEL, 0), const((1, D_MODEL)), const((D_MODEL, D_PROJ)),
            const((D_PROJ_T, D_MODEL)),
            const(eye2.shape), const(bias.shape), const(expand.shape),
            const((CONV_WIDTH, D_CONV)), const((1, D_CONV)), const((1, LANES)),
            const((1, LANES)), const((1, D_SSM)), const((1, D_SSM)),
        ],
        out_specs=group(D_MIX, 1),
        out_shape=jax.ShapeDtypeStruct((bsz, seq, D_MIX), bf16),
        scratch_shapes=[
            handoff(CHUNK, D_SSM, bf16), handoff(CHUNK, D_CONV, bf16), handoff(CHUNK, D_KV, bf16),
            handoff(CHUNK, LANES, jnp.float32), handoff(D_ATTN, CHUNK, bf16),
            handoff(D_KV, CHUNK, bf16),
            pltpu.VMEM((ROWS_PER_STEP, SUBLANES, D_CONV), jnp.float32),
            pltpu.VMEM((ROWS_PER_STEP, 2 * WINDOW, D_KV), bf16),
            pltpu.VMEM((ROWS_PER_STEP, D_KV, 2 * WINDOW), bf16),
            pltpu.VMEM((ROWS_PER_STEP, SSM_GROUPS, D_STATE, gw), jnp.float32),
        ],
        compiler_params=pltpu.CompilerParams(
            dimension_semantics=("arbitrary",), vmem_limit_bytes=VMEM_LIMIT),
        name="proj_mixer",
    )(sinks, x, norm_mix_w, w_all, w_t, eye2, bias, expand,
      conv_w, conv_b, dt_bias, a_log, d_skip, ssm_norm_w)


def _out_ffn_kernel(x_ref, y_ref, wo_ref, nffn_ref, wg_ref, wu_ref, wd_ref, nfin_ref, o_ref):
    x1 = x_ref[...] + jnp.dot(y_ref[...], wo_ref[...], preferred_element_type=jnp.float32)
    h = _rms_norm(x1, nffn_ref[...]).astype(jnp.bfloat16)
    gate = jnp.dot(h, wg_ref[...], preferred_element_type=jnp.float32)
    up = jnp.dot(h, wu_ref[...], preferred_element_type=jnp.float32)
    act = (_silu(gate) * up).astype(jnp.bfloat16)
    x2 = x1 + jnp.dot(act, wd_ref[...], preferred_element_type=jnp.float32)
    o_ref[...] = _rms_norm(x2, nfin_ref[...])


def _out_ffn(x2d, y2d, w_out, norm_ffn_w, w_gate, w_up, w_down, norm_final_w):
    n_tok = x2d.shape[0]
    tm = TOKEN_TILE
    row = lambda i: (i, 0)
    resident = lambda shape: pl.BlockSpec(shape, lambda i: (0, 0), pipeline_mode=pl.Buffered(1))
    return pl.pallas_call(
        _out_ffn_kernel,
        grid=(n_tok // tm,),
        in_specs=[
            pl.BlockSpec((tm, D_MODEL), row),
            pl.BlockSpec((tm, D_MIX), row),
            resident((D_MIX, D_MODEL)),
            resident((1, D_MODEL)),
            resident((D_MODEL, D_FF)),
            resident((D_MODEL, D_FF)),
            resident((D_FF, D_MODEL)),
            resident((1, D_MODEL)),
        ],
        out_specs=pl.BlockSpec((tm, D_MODEL), row),
        out_shape=jax.ShapeDtypeStruct((n_tok, D_MODEL), jnp.float32),
        compiler_params=pltpu.CompilerParams(
            dimension_semantics=("arbitrary",), vmem_limit_bytes=VMEM_LIMIT),
        name="out_ffn",
    )(x2d, y2d, w_out, norm_ffn_w, w_gate, w_up, w_down, norm_final_w)


def _per_head_lanes(v):
    return jnp.repeat(v, DT_REP).reshape(1, LANES)


def _layer(x, norm_mix_w, w_in, conv_w, conv_b, dt_bias, a_log, d_skip, ssm_norm_w,
           attn_sinks, w_out, norm_ffn_w, w_gate, w_up, w_down, norm_out_w):
    bsz, seq, _ = x.shape
    bf16 = jnp.bfloat16
    s_xbc = D_SSM + D_CONV
    s_dt = s_xbc + SSM_HEADS
    s_q = s_dt + D_ATTN
    s_k = s_q + D_KV
    w_all = jnp.concatenate(
        [w_in[:, :s_xbc], w_in[:, s_q:s_k], jnp.repeat(w_in[:, s_xbc:s_dt], DT_REP, axis=1)],
        axis=1).astype(bf16)
    w_t = jnp.concatenate(
        [w_in[:, s_dt:s_q] * (HEAD_DIM ** -0.5 * LOG2E), w_in[:, s_k:]], axis=1).T.astype(bf16)
    x2d = x.reshape(bsz * seq, D_MODEL)
    y = _proj_mixer(x, norm_mix_w.reshape(1, D_MODEL), w_all, w_t, conv_w,
                    conv_b.reshape(1, D_CONV), _per_head_lanes(dt_bias), _per_head_lanes(a_log),
                    jnp.repeat(d_skip, SSM_HEAD_DIM).reshape(1, D_SSM),
                    ssm_norm_w.reshape(1, D_SSM), attn_sinks)
    out = _out_ffn(x2d, y.reshape(bsz * seq, D_MIX), w_out.astype(bf16),
                   norm_ffn_w.reshape(1, D_MODEL), w_gate.astype(bf16), w_up.astype(bf16),
                   w_down.astype(bf16), norm_out_w.reshape(1, D_MODEL))
    return out.reshape(bsz, seq, D_MODEL)


def kernel(x, norm_mix_w, w_in, conv_w, conv_b, dt_bias, a_log, d_skip, ssm_norm_w,
           attn_sinks, w_out, norm_ffn_w, w_gate, w_up, w_down, norm_final_w):
    depth = norm_mix_w.shape[0]
    assert depth == 1, "final RMSNorm is fused into the single layer's FFN kernel"
    return _layer(x, norm_mix_w[0], w_in[0], conv_w[0], conv_b[0], dt_bias[0], a_log[0],
                  d_skip[0], ssm_norm_w[0], attn_sinks[0], w_out[0], norm_ffn_w[0],
                  w_gate[0], w_up[0], w_down[0], norm_final_w)
```

```python
import functools
import math

import jax
import jax.numpy as jnp
from jax import lax
from jax.experimental import pallas as pl
from jax.experimental.pallas import tpu as pltpu

D_MODEL = 1024
SSM_HEADS = 8
SSM_HEAD_DIM = 64
D_SSM = SSM_HEADS * SSM_HEAD_DIM
SSM_GROUPS = 2
HEADS_PER_GROUP = SSM_HEADS // SSM_GROUPS
D_STATE = 128
CONV_WIDTH = 4
CHUNK = 128
D_BC = SSM_GROUPS * D_STATE
D_CONV = D_SSM + 2 * D_BC
ATTN_HEADS = 8
KV_HEADS = 2
Q_PER_KV = ATTN_HEADS // KV_HEADS
HEAD_DIM = 64
D_ATTN = ATTN_HEADS * HEAD_DIM
D_KV = KV_HEADS * HEAD_DIM
WINDOW = 128
D_MIX = D_SSM + D_ATTN
D_FF = 2816
EPS = 1e-5

LANES = 128
SUBLANES = 8
HEAD_PAIR = LANES // HEAD_DIM
DT_REP = LANES // SSM_HEADS
LOG2E = math.log2(math.e)
MASKED = -1e30
COL_Z = 0
COL_XBC = COL_Z + D_SSM
COL_K = COL_XBC + D_CONV
COL_DT = COL_K + D_KV
D_PROJ = COL_DT + LANES
D_PROJ_T = D_ATTN + D_KV

PIPELINE_LAG = 2
ROWS_PER_STEP = 2
VMEM_LIMIT = 56 * 1024 * 1024

assert CHUNK == WINDOW == D_STATE == LANES and HEAD_DIM == SSM_HEAD_DIM and HEAD_PAIR == 2


def _rms_norm(x, w):
    return x * lax.rsqrt(jnp.mean(x * x, axis=-1, keepdims=True) + EPS) * w


def _silu(x):
    return x * jax.nn.sigmoid(x)


def _bf16_dot(a, b):
    return jnp.dot(a.astype(jnp.bfloat16), b.astype(jnp.bfloat16),
                   preferred_element_type=jnp.float32)


def _bf16_dot_nt(a, b):
    return lax.dot_general(a.astype(jnp.bfloat16), b.astype(jnp.bfloat16),
                           (((1,), (1,)), ((), ())),
                           preferred_element_type=jnp.float32)


def _in_proj_stages(x_ref, nw_ref, w_ref, wt_ref, z_ref, xbc_ref, k_ref, dt_ref, qt_ref, vt_ref):
    rows = ROWS_PER_STEP * CHUNK
    h = _rms_norm(x_ref[...].reshape(rows, D_MODEL), nw_ref[...]).astype(jnp.bfloat16)
    yield
    p = jnp.dot(h, w_ref[...], preferred_element_type=jnp.float32)
    for r in range(ROWS_PER_STEP):
        p_r = p[r * CHUNK:(r + 1) * CHUNK, :]
        z_ref[r] = p_r[:, COL_Z:COL_XBC].astype(z_ref.dtype)
        xbc_ref[r] = p_r[:, COL_XBC:COL_K].astype(xbc_ref.dtype)
        k_ref[r] = p_r[:, COL_K:COL_DT].astype(k_ref.dtype)
        dt_ref[r] = p_r[:, COL_DT:D_PROJ]
    yield
    pt = _bf16_dot_nt(wt_ref[...], h)
    for r in range(ROWS_PER_STEP):
        pt_r = pt[:, r * CHUNK:(r + 1) * CHUNK]
        qt_ref[r] = pt_r[:D_ATTN, :].astype(qt_ref.dtype)
        vt_ref[r] = pt_r[D_ATTN:, :].astype(vt_ref.dtype)


def _split_halves(x, low_half):
    zero = jnp.zeros_like(x)
    return jnp.concatenate([jnp.where(low_half, x, zero), jnp.where(low_half, zero, x)], axis=0)


def _hi_lo(x):
    hi = x.astype(jnp.bfloat16).astype(jnp.float32)
    return hi, x - hi


def _run_interleaved(stage_generators):
    pending = list(stage_generators)
    while pending:
        for gen in list(pending):
            try:
                next(gen)
            except StopIteration:
                pending.remove(gen)


def _layer_kernel(nc, sinks_ref, x_ref, nmix_ref, w_ref, wt_ref,
                  eye_ref, bias_ref, expand_ref,
                  convw_ref, convb_ref, dtb_ref, alog_ref, dskip_ref, nw_ref,
                  xres_ref, wo_ref, nffn_ref, wg_ref, wu_ref, wd_ref, nfin_ref,
                  o_ref, z_buf, xbc_buf, k_buf, dt_buf, qt_buf, vt_buf, ymix_buf,
                  tail_ref, kext_ref, vtext_ref, state_ref):
    s = pl.program_id(0)
    c = jnp.maximum(lax.rem(s - 1, nc), 0)

    @pl.when(s == 0)
    def _():
        for ref in (z_buf, xbc_buf, k_buf, dt_buf, qt_buf, vt_buf, ymix_buf):
            ref[...] = jnp.zeros_like(ref)

    @pl.when(c == 0)
    def _():
        tail_ref[...] = jnp.zeros_like(tail_ref)
        kext_ref[:, 0:WINDOW, :] = jnp.zeros((ROWS_PER_STEP, WINDOW, D_KV), kext_ref.dtype)
        vtext_ref[:, :, 0:WINDOW] = jnp.zeros((ROWS_PER_STEP, D_KV, WINDOW), vtext_ref.dtype)
        state_ref[...] = jnp.zeros_like(state_ref)

    def step(wr):
        rd = 1 - wr
        stages = [_out_ffn_stages(xres_ref, ymix_buf.at[rd], wo_ref, nffn_ref, wg_ref, wu_ref,
                                  wd_ref, nfin_ref, o_ref),
                  _in_proj_stages(x_ref, nmix_ref, w_ref, wt_ref, z_buf.at[wr], xbc_buf.at[wr],
                                  k_buf.at[wr], dt_buf.at[wr], qt_buf.at[wr], vt_buf.at[wr])]
        for r in range(ROWS_PER_STEP):
            stages.append(_attention_stages(
                c, sinks_ref, k_buf.at[rd, r], qt_buf.at[rd, r], vt_buf.at[rd, r], eye_ref,
                bias_ref, ymix_buf.at[wr, r], kext_ref.at[r], vtext_ref.at[r]))
            stages.append(_ssd_stages(
                z_buf.at[rd, r], xbc_buf.at[rd, r], dt_buf.at[rd, r], expand_ref, convw_ref,
                convb_ref, dtb_ref, alog_ref, dskip_ref, nw_ref, ymix_buf.at[wr, r],
                tail_ref.at[r], state_ref.at[r]))
        _run_interleaved(stages)

    parity = lax.rem(s, 2)
    for wr in range(2):
        pl.when(parity == wr)(functools.partial(step, wr))


def _ssd_stages(z_ref, xbc_ref, dt_ref, expand_ref, convw_ref, convb_ref, dtb_ref,
                alog_ref, dskip_ref, nw_ref, o_ref, tail_ref, state_ref):
    cur = xbc_ref[...].astype(jnp.float32)
    tail = tail_ref[...]
    tail_ref[...] = cur[CHUNK - SUBLANES:, :]
    first_rows = lax.broadcasted_iota(jnp.int32, (SUBLANES, D_CONV), 0)
    delayed = [cur]
    for s in range(1, CONV_WIDTH):
        rolled = pltpu.roll(cur, s, 0)
        head = jnp.where(first_rows < s, pltpu.roll(tail, s, 0), rolled[:SUBLANES, :])
        delayed.append(jnp.concatenate([head, rolled[SUBLANES:, :]], axis=0))
    yield

    dt = jax.nn.softplus(dt_ref[...] + dtb_ref[...])
    a_log2 = -jnp.exp(alog_ref[...]) * LOG2E
    a_cum = dt * a_log2
    row = lax.broadcasted_iota(jnp.int32, (CHUNK, LANES), 0)
    shift = 1
    while shift < CHUNK:
        a_cum = a_cum + jnp.where(row >= shift, pltpu.roll(a_cum, shift, 0), 0.0)
        shift *= 2
    a_cum_t = a_cum.T
    a_last = a_cum[CHUNK - 1:CHUNK, :]
    sub = lax.broadcasted_iota(jnp.int32, (CHUNK, LANES), 1) % DT_REP
    packed = jnp.zeros((CHUNK, LANES), jnp.float32)
    for i, val in enumerate((dt, dt * jnp.exp2(a_last - a_cum), jnp.exp2(a_cum))):
        for j, part in enumerate(_hi_lo(val)):
            packed = jnp.where(sub == 2 * i + j, part, packed)
    spread = _bf16_dot(packed, expand_ref[...])
    dt_x = spread[:, :D_SSM]
    dt_end_x = spread[:, D_SSM:2 * D_SSM]
    off_scale = spread[:, 2 * D_SSM:]
    state_scale = off_scale[CHUNK - 1:CHUNK, :]
    yield

    conv = convb_ref[...]
    for s in range(CONV_WIDTH):
        tap = CONV_WIDTH - 1 - s
        conv = conv + convw_ref[tap:tap + 1, :] * delayed[s]
    xbc = _silu(conv)
    xs = xbc[:, :D_SSM]
    b_all = xbc[:, D_SSM:D_SSM + D_BC]
    c_all = xbc[:, D_SSM + D_BC:]
    yield

    xdt = xs * dt_x
    xdt_end = xs * dt_end_x
    gw = HEADS_PER_GROUP * SSM_HEAD_DIM
    cbs, y_offs = [], []
    for g in range(SSM_GROUPS):
        b_g = b_all[:, g * D_STATE:(g + 1) * D_STATE]
        c_g = c_all[:, g * D_STATE:(g + 1) * D_STATE]
        cbs.append(_bf16_dot_nt(c_g, b_g))
        state = state_ref[g]
        y_offs.append(_bf16_dot(c_g, state) * off_scale[:, g * gw:(g + 1) * gw])
        new_states = _bf16_dot(b_g.T, xdt_end[:, g * gw:(g + 1) * gw])
        state_ref[g] = state * state_scale[:, g * gw:(g + 1) * gw] + new_states
    yield

    t_idx = lax.broadcasted_iota(jnp.int32, (CHUNK, CHUNK), 0)
    s_idx = lax.broadcasted_iota(jnp.int32, (CHUNK, CHUNK), 1)
    causal = t_idx >= s_idx
    low_half = s_idx < SSM_HEAD_DIM
    pairs_per_group = HEADS_PER_GROUP // HEAD_PAIR
    y_tiles = []
    for g in range(SSM_GROUPS):
        for pp in range(pairs_per_group):
            pair = g * pairs_per_group + pp
            weights = []
            for u in range(HEAD_PAIR):
                col = (pair * HEAD_PAIR + u) * DT_REP
                seg = a_cum[:, col:col + 1] - a_cum_t[col:col + 1, :]
                weights.append(cbs[g] * jnp.exp2(jnp.where(causal, seg, -jnp.inf)))
            rhs = _split_halves(xdt[:, pair * LANES:(pair + 1) * LANES], low_half)
            y_diag = _bf16_dot(jnp.concatenate(weights, axis=1), rhs)
            y_tiles.append(y_diag + y_offs[g][:, pp * LANES:(pp + 1) * LANES])
        yield

    y = jnp.concatenate(y_tiles, axis=1) + dskip_ref[...] * xs
    y = y * _silu(z_ref[...].astype(jnp.float32))
    for g in range(SSM_GROUPS):
        yg = y[:, g * gw:(g + 1) * gw]
        o_ref[:, g * gw:(g + 1) * gw] = _rms_norm(
            yg, nw_ref[:, g * gw:(g + 1) * gw]).astype(o_ref.dtype)


def _attention_stages(c, sinks_ref, k_ref, qt_ref, vt_ref, eye_ref, bias_ref, o_ref,
                      kext_ref, vtext_ref):
    kext_ref[WINDOW:2 * WINDOW, :] = k_ref[...]
    vtext_ref[:, WINDOW:2 * WINDOW] = vt_ref[...]
    k_ext = kext_ref[...]
    vt_ext = vtext_ref[...]
    kext_ref[0:WINDOW, :] = k_ref[...]
    vtext_ref[:, 0:WINDOW] = vt_ref[...]
    bias_t = bias_ref[jnp.minimum(c, 1)]
    eye2 = eye_ref[...]
    low_kv = lax.broadcasted_iota(jnp.int32, (2 * WINDOW, LANES), 1) < HEAD_DIM
    k_swap = pltpu.roll(k_ext, HEAD_DIM, 1)
    zero_k = jnp.zeros_like(k_ext)
    zero_vt = jnp.zeros((HEAD_DIM, 2 * WINDOW), vt_ext.dtype)
    pairs_per_kv = Q_PER_KV // HEAD_PAIR
    assert pairs_per_kv == 2
    logits, values_t = [], []
    for j in range(KV_HEADS):
        k_lo = jnp.where(low_kv, k_ext if j == 0 else k_swap, zero_k)
        k_hi = jnp.where(low_kv, zero_k, k_swap if j == 0 else k_ext)
        keys = jnp.concatenate([jnp.concatenate([k_lo, bias_t], axis=1),
                                jnp.concatenate([k_hi, bias_t], axis=1)], axis=0)
        rows = slice(j * pairs_per_kv * LANES, (j + 1) * pairs_per_kv * LANES)
        q_t = qt_ref[rows, :]
        queries = jnp.concatenate(
            [jnp.concatenate([q_t[:LANES, :], q_t[LANES:, :]], axis=1), eye2], axis=0)
        logits.append(jnp.dot(keys, queries, preferred_element_type=jnp.float32))
        v_t = vt_ext[j * HEAD_DIM:(j + 1) * HEAD_DIM, :]
        values_t.append(jnp.concatenate([jnp.concatenate([v_t, zero_vt], axis=1),
                                         jnp.concatenate([zero_vt, v_t], axis=1)], axis=0))
    yield

    top_rows = lax.broadcasted_iota(jnp.int32, (LANES, LANES), 0) < HEAD_DIM
    for j in range(KV_HEADS):
        prob_rows, denom_tiles = [], []
        for u in range(HEAD_PAIR):
            prob_tiles, denoms = [], []
            for pp in range(pairs_per_kv):
                s_h = logits[j][u * 2 * WINDOW:(u + 1) * 2 * WINDOW, pp * LANES:(pp + 1) * LANES]
                sink = sinks_ref[(j * pairs_per_kv + pp) * HEAD_PAIR + u] * LOG2E
                m = jnp.maximum(jnp.max(s_h, axis=0, keepdims=True), sink)
                p = jnp.exp2(s_h - m)
                prob_tiles.append(p.astype(jnp.bfloat16))
                denoms.append(jnp.sum(p, axis=0, keepdims=True) + jnp.exp2(sink - m))
            prob_rows.append(jnp.concatenate(prob_tiles, axis=1))
            denom_tiles.append(denoms)
            yield
        probs_t = jnp.concatenate(prob_rows, axis=0)
        out_t = jnp.dot(values_t[j], probs_t, preferred_element_type=jnp.float32)
        for pp in range(pairs_per_kv):
            denom = jnp.where(top_rows, denom_tiles[0][pp], denom_tiles[1][pp])
            out = (out_t[:, pp * LANES:(pp + 1) * LANES] / denom).T
            lo = D_SSM + (j * pairs_per_kv + pp) * LANES
            o_ref[:, lo:lo + LANES] = out.astype(o_ref.dtype)
        yield


def _mixer_tables():
    bf16 = jnp.bfloat16
    eye = jnp.eye(WINDOW, dtype=bf16)
    eye2 = jnp.concatenate([eye, eye], axis=1)
    key = jnp.arange(2 * WINDOW)[:, None]
    qry = jnp.arange(WINDOW)[None, :] + WINDOW
    band = (key <= qry) & (key > qry - WINDOW)
    first = band & (key >= WINDOW)
    bias = jnp.stack([jnp.where(first, 0.0, MASKED), jnp.where(band, 0.0, MASKED)]).astype(bf16)
    lane = jnp.arange(LANES)[:, None]
    out = jnp.arange(3 * D_SSM)[None, :]
    quantity = (lane % DT_REP) // 2
    expand = ((out // D_SSM == quantity) & ((out % D_SSM) // SSM_HEAD_DIM == lane // DT_REP)
              & (lane % DT_REP < 6)).astype(bf16)
    return eye2, bias, expand


def _layer_call(x, norm_mix_w, w_all, w_t, conv_w, conv_b, dt_bias, a_log, d_skip, ssm_norm_w,
                sinks, w_out, norm_ffn_w, w_gate, w_up, w_down, norm_final_w):
    bsz, seq, _ = x.shape
    nc = seq // CHUNK
    n_groups = (bsz // ROWS_PER_STEP) * nc
    eye2, bias, expand = _mixer_tables()
    group = lambda width, lag: pl.BlockSpec(
        (ROWS_PER_STEP, CHUNK, width),
        lambda s: (jnp.clip(s - lag, 0, n_groups - 1) // nc, jnp.clip(s - lag, 0, n_groups - 1) % nc, 0))
    resident = lambda shape: pl.BlockSpec(shape, lambda s: (0,) * len(shape),
                                          pipeline_mode=pl.Buffered(1))
    gw = HEADS_PER_GROUP * SSM_HEAD_DIM
    bf16 = jnp.bfloat16
    handoff = lambda rows, cols, dtype: pltpu.VMEM((2, ROWS_PER_STEP, rows, cols), dtype)
    return pl.pallas_call(
        functools.partial(_layer_kernel, nc),
        grid=(n_groups + PIPELINE_LAG,),
        in_specs=[
            pl.BlockSpec(memory_space=pltpu.SMEM),
            group(D_MODEL, 0), resident((1, D_MODEL)), resident((D_MODEL, D_PROJ)),
            resident((D_PROJ_T, D_MODEL)),
            resident(eye2.shape), resident(bias.shape), resident(expand.shape),
            resident((CONV_WIDTH, D_CONV)), resident((1, D_CONV)), resident((1, LANES)),
            resident((1, LANES)), resident((1, D_SSM)), resident((1, D_SSM)),
            group(D_MODEL, PIPELINE_LAG), resident((D_MIX, D_MODEL)), resident((1, D_MODEL)),
            resident((D_MODEL, D_FF)), resident((D_MODEL, D_FF)), resident((D_FF, D_MODEL)),
            resident((1, D_MODEL)),
        ],
        out_specs=group(D_MODEL, PIPELINE_LAG),
        out_shape=jax.ShapeDtypeStruct((bsz, seq, D_MODEL), jnp.float32),
        scratch_shapes=[
            handoff(CHUNK, D_SSM, bf16), handoff(CHUNK, D_CONV, bf16), handoff(CHUNK, D_KV, bf16),
            handoff(CHUNK, LANES, jnp.float32), handoff(D_ATTN, CHUNK, bf16),
            handoff(D_KV, CHUNK, bf16),
            handoff(CHUNK, D_MIX, bf16),
            pltpu.VMEM((ROWS_PER_STEP, SUBLANES, D_CONV), jnp.float32),
            pltpu.VMEM((ROWS_PER_STEP, 2 * WINDOW, D_KV), bf16),
            pltpu.VMEM((ROWS_PER_STEP, D_KV, 2 * WINDOW), bf16),
            pltpu.VMEM((ROWS_PER_STEP, SSM_GROUPS, D_STATE, gw), jnp.float32),
        ],
        compiler_params=pltpu.CompilerParams(
            dimension_semantics=("arbitrary",), vmem_limit_bytes=VMEM_LIMIT),
        name="layer",
    )(sinks, x, norm_mix_w, w_all, w_t, eye2, bias, expand,
      conv_w, conv_b, dt_bias, a_log, d_skip, ssm_norm_w,
      x, w_out, norm_ffn_w, w_gate, w_up, w_down, norm_final_w)


def _out_ffn_stages(x_ref, ymix_ref, wo_ref, nffn_ref, wg_ref, wu_ref, wd_ref, nfin_ref, o_ref):
    rows = ROWS_PER_STEP * CHUNK
    x1 = x_ref[...].reshape(rows, D_MODEL) + jnp.dot(
        ymix_ref[...].reshape(rows, D_MIX), wo_ref[...], preferred_element_type=jnp.float32)
    h = _rms_norm(x1, nffn_ref[...]).astype(jnp.bfloat16)
    yield
    gate = jnp.dot(h, wg_ref[...], preferred_element_type=jnp.float32)
    yield
    up = jnp.dot(h, wu_ref[...], preferred_element_type=jnp.float32)
    yield
    act = (_silu(gate) * up).astype(jnp.bfloat16)
    yield
    x2 = x1 + jnp.dot(act, wd_ref[...], preferred_element_type=jnp.float32)
    yield
    o_ref[...] = _rms_norm(x2, nfin_ref[...]).reshape(ROWS_PER_STEP, CHUNK, D_MODEL)


def _per_head_lanes(v):
    return jnp.repeat(v, DT_REP).reshape(1, LANES)


def _layer(x, norm_mix_w, w_in, conv_w, conv_b, dt_bias, a_log, d_skip, ssm_norm_w,
           attn_sinks, w_out, norm_ffn_w, w_gate, w_up, w_down, norm_out_w):
    bsz, seq, _ = x.shape
    bf16 = jnp.bfloat16
    s_xbc = D_SSM + D_CONV
    s_dt = s_xbc + SSM_HEADS
    s_q = s_dt + D_ATTN
    s_k = s_q + D_KV
    w_all = jnp.concatenate(
        [w_in[:, :s_xbc], w_in[:, s_q:s_k], jnp.repeat(w_in[:, s_xbc:s_dt], DT_REP, axis=1)],
        axis=1).astype(bf16)
    w_t = jnp.concatenate(
        [w_in[:, s_dt:s_q] * (HEAD_DIM ** -0.5 * LOG2E), w_in[:, s_k:]], axis=1).T.astype(bf16)
    return _layer_call(
        x, norm_mix_w.reshape(1, D_MODEL), w_all, w_t, conv_w, conv_b.reshape(1, D_CONV),
        _per_head_lanes(dt_bias), _per_head_lanes(a_log),
        jnp.repeat(d_skip, SSM_HEAD_DIM).reshape(1, D_SSM), ssm_norm_w.reshape(1, D_SSM),
        attn_sinks, w_out.astype(bf16), norm_ffn_w.reshape(1, D_MODEL), w_gate.astype(bf16),
        w_up.astype(bf16), w_down.astype(bf16), norm_out_w.reshape(1, D_MODEL))


def kernel(x, norm_mix_w, w_in, conv_w, conv_b, dt_bias, a_log, d_skip, ssm_norm_w,
           attn_sinks, w_out, norm_ffn_w, w_gate, w_up, w_down, norm_final_w):
    depth = norm_mix_w.shape[0]
    assert depth == 1, "final RMSNorm is fused into the single layer's FFN kernel"
    return _layer(x, norm_mix_w[0], w_in[0], conv_w[0], conv_b[0], dt_bias[0], a_log[0],
                  d_skip[0], ssm_norm_w[0], attn_sinks[0], w_out[0], norm_ffn_w[0],
                  w_gate[0], w_up[0], w_down[0], norm_final_w)
```

```python
import functools
import math

import jax
import jax.numpy as jnp
import numpy as np
from jax import lax
from jax.experimental import pallas as pl
from jax.experimental.pallas import tpu as pltpu

D_MODEL = 1024
SSM_HEADS = 8
SSM_HEAD_DIM = 64
D_SSM = SSM_HEADS * SSM_HEAD_DIM
SSM_GROUPS = 2
HEADS_PER_GROUP = SSM_HEADS // SSM_GROUPS
D_STATE = 128
CONV_WIDTH = 4
CHUNK = 128
D_BC = SSM_GROUPS * D_STATE
D_CONV = D_SSM + 2 * D_BC
ATTN_HEADS = 8
KV_HEADS = 2
Q_PER_KV = ATTN_HEADS // KV_HEADS
HEAD_DIM = 64
D_ATTN = ATTN_HEADS * HEAD_DIM
D_KV = KV_HEADS * HEAD_DIM
WINDOW = 128
D_MIX = D_SSM + D_ATTN
D_FF = 2816
EPS = 1e-5

LANES = 128
SUBLANES = 8
HEAD_PAIR = LANES // HEAD_DIM
DT_REP = LANES // SSM_HEADS
LOG2E = math.log2(math.e)
MASKED = -1e30
COL_Z = 0
COL_XBC = COL_Z + D_SSM
COL_K = COL_XBC + D_CONV
COL_DT = COL_K + D_KV
D_PROJ = COL_DT + LANES
D_PROJ_T = D_ATTN + D_KV

PIPELINE_LAG = 2
FF_SLICE = 512
ROWS_PER_STEP = 2
VMEM_LIMIT = 56 * 1024 * 1024

assert CHUNK == WINDOW == D_STATE == LANES and HEAD_DIM == SSM_HEAD_DIM and HEAD_PAIR == 2


def _rms_norm(x, w):
    return x * lax.rsqrt(jnp.mean(x * x, axis=-1, keepdims=True) + EPS) * w


def _silu(x):
    return x * jax.nn.sigmoid(x)


def _bf16_dot(a, b):
    return jnp.dot(a.astype(jnp.bfloat16), b.astype(jnp.bfloat16),
                   preferred_element_type=jnp.float32)


def _bf16_dot_nt(a, b):
    return lax.dot_general(a.astype(jnp.bfloat16), b.astype(jnp.bfloat16),
                           (((1,), (1,)), ((), ())),
                           preferred_element_type=jnp.float32)


def _in_proj_stages(x_ref, nw_ref, w_ref, wt_ref, z_ref, xbc_ref, k_ref, dt_ref, qt_ref, vt_ref):
    rows = ROWS_PER_STEP * CHUNK
    h = _rms_norm(x_ref[...].reshape(rows, D_MODEL), nw_ref[...]).astype(jnp.bfloat16)
    yield
    p = jnp.dot(h, w_ref[...], preferred_element_type=jnp.float32)
    for r in range(ROWS_PER_STEP):
        p_r = p[r * CHUNK:(r + 1) * CHUNK, :]
        z_ref[r] = p_r[:, COL_Z:COL_XBC].astype(z_ref.dtype)
        xbc_ref[r] = p_r[:, COL_XBC:COL_K].astype(xbc_ref.dtype)
        k_ref[r] = p_r[:, COL_K:COL_DT].astype(k_ref.dtype)
        dt_ref[r] = p_r[:, COL_DT:D_PROJ]
    yield
    pt = _bf16_dot_nt(wt_ref[...], h)
    for r in range(ROWS_PER_STEP):
        pt_r = pt[:, r * CHUNK:(r + 1) * CHUNK]
        qt_ref[r] = pt_r[:D_ATTN, :].astype(qt_ref.dtype)
        vt_ref[r] = pt_r[D_ATTN:, :].astype(vt_ref.dtype)


def _split_halves(x, low_half):
    zero = jnp.zeros_like(x)
    return jnp.concatenate([jnp.where(low_half, x, zero), jnp.where(low_half, zero, x)], axis=0)


def _hi_lo(x):
    hi = x.astype(jnp.bfloat16).astype(jnp.float32)
    return hi, x - hi


def _run_interleaved(stage_generators):
    pending = list(stage_generators)
    while pending:
        for gen in list(pending):
            try:
                next(gen)
            except StopIteration:
                pending.remove(gen)


def _layer_kernel(nc, sinks_ref, x_ref, nmix_ref, w_ref, wt_ref,
                  eye_ref, bias_ref, expand_ref,
                  convw_ref, convb_ref, dtb_ref, alog_ref, dskip_ref, nw_ref,
                  xres_ref, wo_ref, nffn_ref, wg_ref, wu_ref, wd_ref, nfin_ref,
                  o_ref, z_buf, xbc_buf, k_buf, dt_buf, qt_buf, vt_buf, ymix_buf,
                  tail_ref, kext_ref, vtext_ref, state_ref):
    s = pl.program_id(0)
    c = jnp.maximum(lax.rem(s - 1, nc), 0)

    @pl.when(s == 0)
    def _():
        for ref in (z_buf, xbc_buf, k_buf, dt_buf, qt_buf, vt_buf, ymix_buf):
            ref[...] = jnp.zeros_like(ref)

    @pl.when(c == 0)
    def _():
        tail_ref[...] = jnp.zeros_like(tail_ref)
        kext_ref[:, 0:WINDOW, :] = jnp.zeros((ROWS_PER_STEP, WINDOW, D_KV), kext_ref.dtype)
        vtext_ref[:, :, 0:WINDOW] = jnp.zeros((ROWS_PER_STEP, D_KV, WINDOW), vtext_ref.dtype)
        state_ref[...] = jnp.zeros_like(state_ref)

    def step(wr):
        rd = 1 - wr
        stages = []
        for r in range(ROWS_PER_STEP):
            stages.append(_attention_stages(
                c, sinks_ref, k_buf.at[rd, r], qt_buf.at[rd, r], vt_buf.at[rd, r], eye_ref,
                bias_ref, ymix_buf.at[wr, r], kext_ref.at[r], vtext_ref.at[r]))
            stages.append(_ssd_stages(
                z_buf.at[rd, r], xbc_buf.at[rd, r], dt_buf.at[rd, r], expand_ref, convw_ref,
                convb_ref, dtb_ref, alog_ref, dskip_ref, nw_ref, ymix_buf.at[wr, r],
                tail_ref.at[r], state_ref.at[r]))
        stages.append(_in_proj_stages(x_ref, nmix_ref, w_ref, wt_ref, z_buf.at[wr], xbc_buf.at[wr],
                                      k_buf.at[wr], dt_buf.at[wr], qt_buf.at[wr], vt_buf.at[wr]))
        stages.append(_out_ffn_stages(xres_ref, ymix_buf.at[rd], wo_ref, nffn_ref, wg_ref, wu_ref,
                                      wd_ref, nfin_ref, o_ref))
        _run_interleaved(stages)

    parity = lax.rem(s, 2)
    for wr in range(2):
        pl.when(parity == wr)(functools.partial(step, wr))


def _ssd_stages(z_ref, xbc_ref, dt_ref, expand_ref, convw_ref, convb_ref, dtb_ref,
                alog_ref, dskip_ref, nw_ref, o_ref, tail_ref, state_ref):
    cur = xbc_ref[...].astype(jnp.float32)
    tail = tail_ref[...]
    tail_ref[...] = cur[CHUNK - SUBLANES:, :]
    first_rows = lax.broadcasted_iota(jnp.int32, (SUBLANES, D_CONV), 0)
    delayed = [cur]
    for s in range(1, CONV_WIDTH):
        rolled = pltpu.roll(cur, s, 0)
        head = jnp.where(first_rows < s, pltpu.roll(tail, s, 0), rolled[:SUBLANES, :])
        delayed.append(jnp.concatenate([head, rolled[SUBLANES:, :]], axis=0))
    yield

    dt = jax.nn.softplus(dt_ref[...] + dtb_ref[...])
    a_log2 = -jnp.exp(alog_ref[...]) * LOG2E
    a_cum = dt * a_log2
    row = lax.broadcasted_iota(jnp.int32, (CHUNK, LANES), 0)
    shift = 1
    while shift < CHUNK:
        a_cum = a_cum + jnp.where(row >= shift, pltpu.roll(a_cum, shift, 0), 0.0)
        shift *= 2
    a_cum_t = a_cum.T
    a_last = a_cum[CHUNK - 1:CHUNK, :]
    sub = lax.broadcasted_iota(jnp.int32, (CHUNK, LANES), 1) % DT_REP
    packed = jnp.zeros((CHUNK, LANES), jnp.float32)
    for i, val in enumerate((dt, dt * jnp.exp2(a_last - a_cum), jnp.exp2(a_cum))):
        for j, part in enumerate(_hi_lo(val)):
            packed = jnp.where(sub == 2 * i + j, part, packed)
    spread = _bf16_dot(packed, expand_ref[...])
    dt_x = spread[:, :D_SSM]
    dt_end_x = spread[:, D_SSM:2 * D_SSM]
    off_scale = spread[:, 2 * D_SSM:]
    state_scale = off_scale[CHUNK - 1:CHUNK, :]
    yield

    conv = convb_ref[...]
    for s in range(CONV_WIDTH):
        tap = CONV_WIDTH - 1 - s
        conv = conv + convw_ref[tap:tap + 1, :] * delayed[s]
    xbc = _silu(conv)
    xs = xbc[:, :D_SSM]
    b_all = xbc[:, D_SSM:D_SSM + D_BC]
    c_all = xbc[:, D_SSM + D_BC:]
    yield

    xdt = xs * dt_x
    xdt_end = xs * dt_end_x
    gw = HEADS_PER_GROUP * SSM_HEAD_DIM
    cbs, y_offs = [], []
    for g in range(SSM_GROUPS):
        b_g = b_all[:, g * D_STATE:(g + 1) * D_STATE]
        c_g = c_all[:, g * D_STATE:(g + 1) * D_STATE]
        cbs.append(_bf16_dot_nt(c_g, b_g))
        state = state_ref[g]
        y_offs.append(_bf16_dot(c_g, state) * off_scale[:, g * gw:(g + 1) * gw])
        new_states = _bf16_dot(b_g.T, xdt_end[:, g * gw:(g + 1) * gw])
        state_ref[g] = state * state_scale[:, g * gw:(g + 1) * gw] + new_states
    yield

    t_idx = lax.broadcasted_iota(jnp.int32, (CHUNK, CHUNK), 0)
    s_idx = lax.broadcasted_iota(jnp.int32, (CHUNK, CHUNK), 1)
    causal = t_idx >= s_idx
    low_half = s_idx < SSM_HEAD_DIM
    pairs_per_group = HEADS_PER_GROUP // HEAD_PAIR
    y_tiles = []
    for g in range(SSM_GROUPS):
        for pp in range(pairs_per_group):
            pair = g * pairs_per_group + pp
            weights = []
            for u in range(HEAD_PAIR):
                col = (pair * HEAD_PAIR + u) * DT_REP
                seg = a_cum[:, col:col + 1] - a_cum_t[col:col + 1, :]
                weights.append(cbs[g] * jnp.exp2(jnp.where(causal, seg, -jnp.inf)))
            rhs = _split_halves(xdt[:, pair * LANES:(pair + 1) * LANES], low_half)
            y_diag = _bf16_dot(jnp.concatenate(weights, axis=1), rhs)
            y_tiles.append(y_diag + y_offs[g][:, pp * LANES:(pp + 1) * LANES])
        yield

    y = jnp.concatenate(y_tiles, axis=1) + dskip_ref[...] * xs
    y = y * _silu(z_ref[...].astype(jnp.float32))
    for g in range(SSM_GROUPS):
        yg = y[:, g * gw:(g + 1) * gw]
        o_ref[:, g * gw:(g + 1) * gw] = _rms_norm(
            yg, nw_ref[:, g * gw:(g + 1) * gw]).astype(o_ref.dtype)


def _attention_stages(c, sinks_ref, k_ref, qt_ref, vt_ref, eye_ref, bias_ref, o_ref,
                      kext_ref, vtext_ref):
    kext_ref[WINDOW:2 * WINDOW, :] = k_ref[...]
    vtext_ref[:, WINDOW:2 * WINDOW] = vt_ref[...]
    k_ext = kext_ref[...]
    vt_ext = vtext_ref[...]
    kext_ref[0:WINDOW, :] = k_ref[...]
    vtext_ref[:, 0:WINDOW] = vt_ref[...]
    bias_t = bias_ref[jnp.minimum(c, 1)]
    eye2 = eye_ref[...]
    low_kv = lax.broadcasted_iota(jnp.int32, (2 * WINDOW, LANES), 1) < HEAD_DIM
    k_swap = pltpu.roll(k_ext, HEAD_DIM, 1)
    zero_k = jnp.zeros_like(k_ext)
    zero_vt = jnp.zeros((HEAD_DIM, 2 * WINDOW), vt_ext.dtype)
    pairs_per_kv = Q_PER_KV // HEAD_PAIR
    assert pairs_per_kv == 2
    logits, values_t = [], []
    for j in range(KV_HEADS):
        k_lo = jnp.where(low_kv, k_ext if j == 0 else k_swap, zero_k)
        k_hi = jnp.where(low_kv, zero_k, k_swap if j == 0 else k_ext)
        keys = jnp.concatenate([jnp.concatenate([k_lo, bias_t], axis=1),
                                jnp.concatenate([k_hi, bias_t], axis=1)], axis=0)
        rows = slice(j * pairs_per_kv * LANES, (j + 1) * pairs_per_kv * LANES)
        q_t = qt_ref[rows, :]
        queries = jnp.concatenate(
            [jnp.concatenate([q_t[:LANES, :], q_t[LANES:, :]], axis=1), eye2], axis=0)
        logits.append(jnp.dot(keys, queries, preferred_element_type=jnp.float32))
        v_t = vt_ext[j * HEAD_DIM:(j + 1) * HEAD_DIM, :]
        values_t.append(jnp.concatenate([jnp.concatenate([v_t, zero_vt], axis=1),
                                         jnp.concatenate([zero_vt, v_t], axis=1)], axis=0))
    yield

    top_rows = lax.broadcasted_iota(jnp.int32, (LANES, LANES), 0) < HEAD_DIM
    for j in range(KV_HEADS):
        prob_rows, denom_tiles = [], []
        for u in range(HEAD_PAIR):
            prob_tiles, denoms = [], []
            for pp in range(pairs_per_kv):
                s_h = logits[j][u * 2 * WINDOW:(u + 1) * 2 * WINDOW, pp * LANES:(pp + 1) * LANES]
                sink = sinks_ref[(j * pairs_per_kv + pp) * HEAD_PAIR + u] * LOG2E
                m = jnp.maximum(jnp.max(s_h, axis=0, keepdims=True), sink)
                p = jnp.exp2(s_h - m)
                prob_tiles.append(p.astype(jnp.bfloat16))
                denoms.append(jnp.sum(p, axis=0, keepdims=True) + jnp.exp2(sink - m))
            prob_rows.append(jnp.concatenate(prob_tiles, axis=1))
            denom_tiles.append(denoms)
            yield
        probs_t = jnp.concatenate(prob_rows, axis=0)
        out_t = jnp.dot(values_t[j], probs_t, preferred_element_type=jnp.float32)
        for pp in range(pairs_per_kv):
            denom = jnp.where(top_rows, denom_tiles[0][pp], denom_tiles[1][pp])
            out = (out_t[:, pp * LANES:(pp + 1) * LANES] / denom).T
            lo = D_SSM + (j * pairs_per_kv + pp) * LANES
            o_ref[:, lo:lo + LANES] = out.astype(o_ref.dtype)
        yield


def _mixer_tables():
    eye = np.eye(WINDOW, dtype=np.float32)
    eye2 = np.concatenate([eye, eye], axis=1)
    key = np.arange(2 * WINDOW)[:, None]
    qry = np.arange(WINDOW)[None, :] + WINDOW
    band = (key <= qry) & (key > qry - WINDOW)
    first = band & (key >= WINDOW)
    bias = np.stack([np.where(first, 0.0, MASKED), np.where(band, 0.0, MASKED)])
    lane = np.arange(LANES)[:, None]
    out = np.arange(3 * D_SSM)[None, :]
    quantity = (lane % DT_REP) // 2
    expand = ((out // D_SSM == quantity) & ((out % D_SSM) // SSM_HEAD_DIM == lane // DT_REP)
              & (lane % DT_REP < 6))
    return tuple(jnp.asarray(t.astype(np.float32), dtype=jnp.bfloat16)
                 for t in (eye2, bias, expand))


def _layer_call(x, norm_mix_w, w_all, w_t, conv_w, conv_b, dt_bias, a_log, d_skip, ssm_norm_w,
                sinks, w_out, norm_ffn_w, w_gate, w_up, w_down, norm_final_w):
    bsz, seq, _ = x.shape
    nc = seq // CHUNK
    n_groups = (bsz // ROWS_PER_STEP) * nc
    eye2, bias, expand = _mixer_tables()
    group = lambda width, lag: pl.BlockSpec(
        (ROWS_PER_STEP, CHUNK, width),
        lambda s: (jnp.clip(s - lag, 0, n_groups - 1) // nc, jnp.clip(s - lag, 0, n_groups - 1) % nc, 0))
    resident = lambda shape: pl.BlockSpec(shape, lambda s: (0,) * len(shape),
                                          pipeline_mode=pl.Buffered(1))
    gw = HEADS_PER_GROUP * SSM_HEAD_DIM
    bf16 = jnp.bfloat16
    handoff = lambda rows, cols, dtype: pltpu.VMEM((2, ROWS_PER_STEP, rows, cols), dtype)
    return pl.pallas_call(
        functools.partial(_layer_kernel, nc),
        grid=(n_groups + PIPELINE_LAG,),
        in_specs=[
            pl.BlockSpec(memory_space=pltpu.SMEM),
            group(D_MODEL, 0), resident((1, D_MODEL)), resident((D_MODEL, D_PROJ)),
            resident((D_PROJ_T, D_MODEL)),
            resident(eye2.shape), resident(bias.shape), resident(expand.shape),
            resident((CONV_WIDTH, D_CONV)), resident((1, D_CONV)), resident((1, LANES)),
            resident((1, LANES)), resident((1, D_SSM)), resident((1, D_SSM)),
            group(D_MODEL, PIPELINE_LAG), resident((D_MIX, D_MODEL)), resident((1, D_MODEL)),
            resident((D_MODEL, D_FF)), resident((D_MODEL, D_FF)), resident((D_FF, D_MODEL)),
            resident((1, D_MODEL)),
        ],
        out_specs=group(D_MODEL, PIPELINE_LAG),
        out_shape=jax.ShapeDtypeStruct((bsz, seq, D_MODEL), jnp.float32),
        scratch_shapes=[
            handoff(CHUNK, D_SSM, bf16), handoff(CHUNK, D_CONV, bf16), handoff(CHUNK, D_KV, bf16),
            handoff(CHUNK, LANES, jnp.float32), handoff(D_ATTN, CHUNK, bf16),
            handoff(D_KV, CHUNK, bf16),
            handoff(CHUNK, D_MIX, bf16),
            pltpu.VMEM((ROWS_PER_STEP, SUBLANES, D_CONV), jnp.float32),
            pltpu.VMEM((ROWS_PER_STEP, 2 * WINDOW, D_KV), bf16),
            pltpu.VMEM((ROWS_PER_STEP, D_KV, 2 * WINDOW), bf16),
            pltpu.VMEM((ROWS_PER_STEP, SSM_GROUPS, D_STATE, gw), jnp.float32),
        ],
        compiler_params=pltpu.CompilerParams(
            dimension_semantics=("arbitrary",), vmem_limit_bytes=VMEM_LIMIT),
        name="layer",
    )(sinks, x, norm_mix_w, w_all, w_t, eye2, bias, expand,
      conv_w, conv_b, dt_bias, a_log, d_skip, ssm_norm_w,
      x, w_out, norm_ffn_w, w_gate, w_up, w_down, norm_final_w)


def _out_ffn_stages(x_ref, ymix_ref, wo_ref, nffn_ref, wg_ref, wu_ref, wd_ref, nfin_ref, o_ref):
    rows = ROWS_PER_STEP * CHUNK
    x1 = x_ref[...].reshape(rows, D_MODEL) + jnp.dot(
        ymix_ref[...].reshape(rows, D_MIX), wo_ref[...], preferred_element_type=jnp.float32)
    h = _rms_norm(x1, nffn_ref[...]).astype(jnp.bfloat16)
    yield
    acts = []
    for lo in range(0, D_FF, FF_SLICE):
        hi = min(lo + FF_SLICE, D_FF)
        gate = jnp.dot(h, wg_ref[:, lo:hi], preferred_element_type=jnp.float32)
        up = jnp.dot(h, wu_ref[:, lo:hi], preferred_element_type=jnp.float32)
        acts.append((_silu(gate) * up).astype(jnp.bfloat16))
        yield
    x2 = x1 + jnp.dot(jnp.concatenate(acts, axis=1), wd_ref[...],
                      preferred_element_type=jnp.float32)
    yield
    o_ref[...] = _rms_norm(x2, nfin_ref[...]).reshape(ROWS_PER_STEP, CHUNK, D_MODEL)


def _per_head_lanes(v):
    return jnp.repeat(v, DT_REP).reshape(1, LANES)


def _layer(x, norm_mix_w, w_in, conv_w, conv_b, dt_bias, a_log, d_skip, ssm_norm_w,
           attn_sinks, w_out, norm_ffn_w, w_gate, w_up, w_down, norm_out_w):
    bsz, seq, _ = x.shape
    bf16 = jnp.bfloat16
    s_xbc = D_SSM + D_CONV
    s_dt = s_xbc + SSM_HEADS
    s_q = s_dt + D_ATTN
    s_k = s_q + D_KV
    w_all = jnp.concatenate(
        [w_in[:, :s_xbc], w_in[:, s_q:s_k], jnp.repeat(w_in[:, s_xbc:s_dt], DT_REP, axis=1)],
        axis=1).astype(bf16)
    w_t = jnp.concatenate(
        [w_in[:, s_dt:s_q] * (HEAD_DIM ** -0.5 * LOG2E), w_in[:, s_k:]], axis=1).T.astype(bf16)
    return _layer_call(
        x, norm_mix_w.reshape(1, D_MODEL), w_all, w_t, conv_w, conv_b.reshape(1, D_CONV),
        _per_head_lanes(dt_bias), _per_head_lanes(a_log),
        jnp.repeat(d_skip, SSM_HEAD_DIM).reshape(1, D_SSM), ssm_norm_w.reshape(1, D_SSM),
        attn_sinks, w_out.astype(bf16), norm_ffn_w.reshape(1, D_MODEL), w_gate.astype(bf16),
        w_up.astype(bf16), w_down.astype(bf16), norm_out_w.reshape(1, D_MODEL))


def kernel(x, norm_mix_w, w_in, conv_w, conv_b, dt_bias, a_log, d_skip, ssm_norm_w,
           attn_sinks, w_out, norm_ffn_w, w_gate, w_up, w_down, norm_final_w):
    depth = norm_mix_w.shape[0]
    assert depth == 1, "final RMSNorm is fused into the single layer's FFN kernel"
    return _layer(x, norm_mix_w[0], w_in[0], conv_w[0], conv_b[0], dt_bias[0], a_log[0],
                  d_skip[0], ssm_norm_w[0], attn_sinks[0], w_out[0], norm_ffn_w[0],
                  w_gate[0], w_up[0], w_down[0], norm_final_w)
```

```python
import functools
import math

import jax
import jax.numpy as jnp
import numpy as np
from jax import lax
from jax.experimental import pallas as pl
from jax.experimental.pallas import tpu as pltpu

D_MODEL = 1024
SSM_HEADS = 8
SSM_HEAD_DIM = 64
D_SSM = SSM_HEADS * SSM_HEAD_DIM
SSM_GROUPS = 2
HEADS_PER_GROUP = SSM_HEADS // SSM_GROUPS
D_STATE = 128
CONV_WIDTH = 4
CHUNK = 128
D_BC = SSM_GROUPS * D_STATE
D_CONV = D_SSM + 2 * D_BC
ATTN_HEADS = 8
KV_HEADS = 2
Q_PER_KV = ATTN_HEADS // KV_HEADS
HEAD_DIM = 64
D_ATTN = ATTN_HEADS * HEAD_DIM
D_KV = KV_HEADS * HEAD_DIM
WINDOW = 128
D_MIX = D_SSM + D_ATTN
D_FF = 2816
EPS = 1e-5

LANES = 128
SUBLANES = 8
HEAD_PAIR = LANES // HEAD_DIM
DT_REP = LANES // SSM_HEADS
LOG2E = math.log2(math.e)
MASKED = -1e30
COL_Z = 0
COL_XBC = COL_Z + D_SSM
COL_K = COL_XBC + D_CONV
COL_DT = COL_K + D_KV
D_PROJ = COL_DT + LANES
D_PROJ_T = D_ATTN + D_KV

PIPELINE_LAG = 2
FF_SLICE = 512
ROWS_PER_STEP = 2
VMEM_LIMIT = 56 * 1024 * 1024

assert CHUNK == WINDOW == D_STATE == LANES and HEAD_DIM == SSM_HEAD_DIM and HEAD_PAIR == 2


def _rms_norm(x, w):
    return x * lax.rsqrt(jnp.mean(x * x, axis=-1, keepdims=True) + EPS) * w


def _silu(x):
    return x * jax.nn.sigmoid(x)


def _bf16_dot(a, b):
    return jnp.dot(a.astype(jnp.bfloat16), b.astype(jnp.bfloat16),
                   preferred_element_type=jnp.float32)


def _bf16_dot_nt(a, b):
    return lax.dot_general(a.astype(jnp.bfloat16), b.astype(jnp.bfloat16),
                           (((1,), (1,)), ((), ())),
                           preferred_element_type=jnp.float32)


def _in_proj_stages(x_ref, nw_ref, w_ref, wt_ref, z_ref, xbc_ref, k_ref, dt_ref, qt_ref, vt_ref):
    rows = ROWS_PER_STEP * CHUNK
    h = _rms_norm(x_ref[...].reshape(rows, D_MODEL), nw_ref[...]).astype(jnp.bfloat16)
    yield
    p = jnp.dot(h, w_ref[...], preferred_element_type=jnp.float32)
    for r in range(ROWS_PER_STEP):
        p_r = p[r * CHUNK:(r + 1) * CHUNK, :]
        z_ref[r] = p_r[:, COL_Z:COL_XBC].astype(z_ref.dtype)
        xbc_ref[r] = p_r[:, COL_XBC:COL_K].astype(xbc_ref.dtype)
        k_ref[r] = p_r[:, COL_K:COL_DT].astype(k_ref.dtype)
        dt_ref[r] = p_r[:, COL_DT:D_PROJ]
    yield
    pt = _bf16_dot_nt(wt_ref[...], h)
    for r in range(ROWS_PER_STEP):
        pt_r = pt[:, r * CHUNK:(r + 1) * CHUNK]
        qt_ref[r] = pt_r[:D_ATTN, :].astype(qt_ref.dtype)
        vt_ref[r] = pt_r[D_ATTN:, :].astype(vt_ref.dtype)


def _split_halves(x, low_half):
    zero = jnp.zeros_like(x)
    return jnp.concatenate([jnp.where(low_half, x, zero), jnp.where(low_half, zero, x)], axis=0)


def _hi_lo(x):
    hi = x.astype(jnp.bfloat16).astype(jnp.float32)
    return hi, x - hi


def _run_interleaved(stage_generators):
    pending = list(stage_generators)
    while pending:
        for gen in list(pending):
            try:
                next(gen)
            except StopIteration:
                pending.remove(gen)


def _layer_kernel(nc, n_groups, sinks_ref, x_ref, nmix_ref, w_ref, wt_ref,
                  eye_ref, bias_ref, expand_ref,
                  convw_ref, convb_ref, dtb_ref, alog_ref, dskip_ref, nw_ref,
                  xres_ref, wo_ref, nffn_ref, wg_ref, wu_ref, wd_ref, nfin_ref,
                  o_ref, z_buf, xbc_buf, k_buf, dt_buf, qt_buf, vt_buf, ymix_buf,
                  tail_ref, kext_ref, vtext_ref, state_ref):
    assert PIPELINE_LAG == 2 and n_groups > 2 * PIPELINE_LAG
    s = pl.program_id(0)
    c = jnp.maximum(lax.rem(s - 1, nc), 0)

    @pl.when((c == 0) & (s >= 1))
    def _():
        tail_ref[...] = jnp.zeros_like(tail_ref)
        kext_ref[:, 0:WINDOW, :] = jnp.zeros((ROWS_PER_STEP, WINDOW, D_KV), kext_ref.dtype)
        vtext_ref[:, :, 0:WINDOW] = jnp.zeros((ROWS_PER_STEP, D_KV, WINDOW), vtext_ref.dtype)
        state_ref[...] = jnp.zeros_like(state_ref)

    def step(wr, project=True, mix=True, ffn=True):
        rd = 1 - wr
        stages = []
        for r in range(ROWS_PER_STEP if mix else 0):
            stages.append(_attention_stages(
                c, sinks_ref, k_buf.at[rd, r], qt_buf.at[rd, r], vt_buf.at[rd, r], eye_ref,
                bias_ref, ymix_buf.at[wr, r], kext_ref.at[r], vtext_ref.at[r]))
            stages.append(_ssd_stages(
                z_buf.at[rd, r], xbc_buf.at[rd, r], dt_buf.at[rd, r], expand_ref, convw_ref,
                convb_ref, dtb_ref, alog_ref, dskip_ref, nw_ref, ymix_buf.at[wr, r],
                tail_ref.at[r], state_ref.at[r]))
        if project:
            stages.append(_in_proj_stages(
                x_ref, nmix_ref, w_ref, wt_ref, z_buf.at[wr], xbc_buf.at[wr], k_buf.at[wr],
                dt_buf.at[wr], qt_buf.at[wr], vt_buf.at[wr]))
        if ffn:
            stages.append(_out_ffn_stages(xres_ref, ymix_buf.at[rd], wo_ref, nffn_ref, wg_ref,
                                          wu_ref, wd_ref, nfin_ref, o_ref))
        _run_interleaved(stages)

    last = n_groups + PIPELINE_LAG - 1
    pl.when(s == 0)(functools.partial(step, 0, mix=False, ffn=False))
    pl.when(s == 1)(functools.partial(step, 1, ffn=False))
    pl.when(s == last - 1)(functools.partial(step, (last - 1) % 2, project=False))
    pl.when(s == last)(functools.partial(step, last % 2, project=False, mix=False))
    steady = (s >= PIPELINE_LAG) & (s < n_groups)
    for wr in range(2):
        pl.when(steady & (lax.rem(s, 2) == wr))(functools.partial(step, wr))


def _ssd_stages(z_ref, xbc_ref, dt_ref, expand_ref, convw_ref, convb_ref, dtb_ref,
                alog_ref, dskip_ref, nw_ref, o_ref, tail_ref, state_ref):
    cur = xbc_ref[...].astype(jnp.float32)
    tail = tail_ref[...]
    tail_ref[...] = cur[CHUNK - SUBLANES:, :]
    first_rows = lax.broadcasted_iota(jnp.int32, (SUBLANES, D_CONV), 0)
    delayed = [cur]
    for s in range(1, CONV_WIDTH):
        rolled = pltpu.roll(cur, s, 0)
        head = jnp.where(first_rows < s, pltpu.roll(tail, s, 0), rolled[:SUBLANES, :])
        delayed.append(jnp.concatenate([head, rolled[SUBLANES:, :]], axis=0))
    yield

    dt = jax.nn.softplus(dt_ref[...] + dtb_ref[...])
    a_log2 = -jnp.exp(alog_ref[...]) * LOG2E
    a_cum = dt * a_log2
    row = lax.broadcasted_iota(jnp.int32, (CHUNK, LANES), 0)
    shift = 1
    while shift < CHUNK:
        a_cum = a_cum + jnp.where(row >= shift, pltpu.roll(a_cum, shift, 0), 0.0)
        shift *= 2
    a_cum_t = a_cum.T
    a_last = a_cum[CHUNK - 1:CHUNK, :]
    sub = lax.broadcasted_iota(jnp.int32, (CHUNK, LANES), 1) % DT_REP
    packed = jnp.zeros((CHUNK, LANES), jnp.float32)
    for i, val in enumerate((dt, dt * jnp.exp2(a_last - a_cum), jnp.exp2(a_cum))):
        for j, part in enumerate(_hi_lo(val)):
            packed = jnp.where(sub == 2 * i + j, part, packed)
    spread = _bf16_dot(packed, expand_ref[...])
    dt_x = spread[:, :D_SSM]
    dt_end_x = spread[:, D_SSM:2 * D_SSM]
    off_scale = spread[:, 2 * D_SSM:]
    state_scale = off_scale[CHUNK - 1:CHUNK, :]
    yield

    conv = convb_ref[...]
    for s in range(CONV_WIDTH):
        tap = CONV_WIDTH - 1 - s
        conv = conv + convw_ref[tap:tap + 1, :] * delayed[s]
    xbc = _silu(conv)
    xs = xbc[:, :D_SSM]
    b_all = xbc[:, D_SSM:D_SSM + D_BC]
    c_all = xbc[:, D_SSM + D_BC:]
    yield

    xdt = xs * dt_x
    xdt_end = xs * dt_end_x
    gw = HEADS_PER_GROUP * SSM_HEAD_DIM
    cbs, y_offs = [], []
    for g in range(SSM_GROUPS):
        b_g = b_all[:, g * D_STATE:(g + 1) * D_STATE]
        c_g = c_all[:, g * D_STATE:(g + 1) * D_STATE]
        cbs.append(_bf16_dot_nt(c_g, b_g))
        state = state_ref[g]
        y_offs.append(_bf16_dot(c_g, state) * off_scale[:, g * gw:(g + 1) * gw])
        new_states = _bf16_dot(b_g.T, xdt_end[:, g * gw:(g + 1) * gw])
        state_ref[g] = state * state_scale[:, g * gw:(g + 1) * gw] + new_states
    yield

    t_idx = lax.broadcasted_iota(jnp.int32, (CHUNK, CHUNK), 0)
    s_idx = lax.broadcasted_iota(jnp.int32, (CHUNK, CHUNK), 1)
    causal = t_idx >= s_idx
    low_half = s_idx < SSM_HEAD_DIM
    pairs_per_group = HEADS_PER_GROUP // HEAD_PAIR
    y_tiles = []
    for g in range(SSM_GROUPS):
        for pp in range(pairs_per_group):
            pair = g * pairs_per_group + pp
            weights = []
            for u in range(HEAD_PAIR):
                col = (pair * HEAD_PAIR + u) * DT_REP
                seg = a_cum[:, col:col + 1] - a_cum_t[col:col + 1, :]
                weights.append(cbs[g] * jnp.exp2(jnp.where(causal, seg, -jnp.inf)))
            rhs = _split_halves(xdt[:, pair * LANES:(pair + 1) * LANES], low_half)
            y_diag = _bf16_dot(jnp.concatenate(weights, axis=1), rhs)
            y_tiles.append(y_diag + y_offs[g][:, pp * LANES:(pp + 1) * LANES])
        yield

    y = jnp.concatenate(y_tiles, axis=1) + dskip_ref[...] * xs
    y = y * _silu(z_ref[...].astype(jnp.float32))
    for g in range(SSM_GROUPS):
        yg = y[:, g * gw:(g + 1) * gw]
        o_ref[:, g * gw:(g + 1) * gw] = _rms_norm(
            yg, nw_ref[:, g * gw:(g + 1) * gw]).astype(o_ref.dtype)


def _attention_stages(c, sinks_ref, k_ref, qt_ref, vt_ref, eye_ref, bias_ref, o_ref,
                      kext_ref, vtext_ref):
    kext_ref[WINDOW:2 * WINDOW, :] = k_ref[...]
    vtext_ref[:, WINDOW:2 * WINDOW] = vt_ref[...]
    k_ext = kext_ref[...]
    vt_ext = vtext_ref[...]
    kext_ref[0:WINDOW, :] = k_ref[...]
    vtext_ref[:, 0:WINDOW] = vt_ref[...]
    bias_t = bias_ref[jnp.minimum(c, 1)]
    eye2 = eye_ref[...]
    low_kv = lax.broadcasted_iota(jnp.int32, (2 * WINDOW, LANES), 1) < HEAD_DIM
    k_swap = pltpu.roll(k_ext, HEAD_DIM, 1)
    zero_k = jnp.zeros_like(k_ext)
    zero_vt = jnp.zeros((HEAD_DIM, 2 * WINDOW), vt_ext.dtype)
    pairs_per_kv = Q_PER_KV // HEAD_PAIR
    assert pairs_per_kv == 2
    logits, values_t = [], []
    for j in range(KV_HEADS):
        k_lo = jnp.where(low_kv, k_ext if j == 0 else k_swap, zero_k)
        k_hi = jnp.where(low_kv, zero_k, k_swap if j == 0 else k_ext)
        keys = jnp.concatenate([jnp.concatenate([k_lo, bias_t], axis=1),
                                jnp.concatenate([k_hi, bias_t], axis=1)], axis=0)
        rows = slice(j * pairs_per_kv * LANES, (j + 1) * pairs_per_kv * LANES)
        q_t = qt_ref[rows, :]
        queries = jnp.concatenate(
            [jnp.concatenate([q_t[:LANES, :], q_t[LANES:, :]], axis=1), eye2], axis=0)
        logits.append(jnp.dot(keys, queries, preferred_element_type=jnp.float32))
        v_t = vt_ext[j * HEAD_DIM:(j + 1) * HEAD_DIM, :]
        values_t.append(jnp.concatenate([jnp.concatenate([v_t, zero_vt], axis=1),
                                         jnp.concatenate([zero_vt, v_t], axis=1)], axis=0))
    yield

    top_rows = lax.broadcasted_iota(jnp.int32, (LANES, LANES), 0) < HEAD_DIM
    for j in range(KV_HEADS):
        prob_rows, denom_tiles = [], []
        for u in range(HEAD_PAIR):
            prob_tiles, denoms = [], []
            for pp in range(pairs_per_kv):
                s_h = logits[j][u * 2 * WINDOW:(u + 1) * 2 * WINDOW, pp * LANES:(pp + 1) * LANES]
                sink = sinks_ref[(j * pairs_per_kv + pp) * HEAD_PAIR + u] * LOG2E
                m = jnp.maximum(jnp.max(s_h, axis=0, keepdims=True), sink)
                p = jnp.exp2(s_h - m)
                prob_tiles.append(p.astype(jnp.bfloat16))
                denoms.append(jnp.sum(p, axis=0, keepdims=True) + jnp.exp2(sink - m))
            prob_rows.append(jnp.concatenate(prob_tiles, axis=1))
            denom_tiles.append(denoms)
            yield
        probs_t = jnp.concatenate(prob_rows, axis=0)
        out_t = jnp.dot(values_t[j], probs_t, preferred_element_type=jnp.float32)
        for pp in range(pairs_per_kv):
            denom = jnp.where(top_rows, denom_tiles[0][pp], denom_tiles[1][pp])
            out = (out_t[:, pp * LANES:(pp + 1) * LANES] / denom).T
            lo = D_SSM + (j * pairs_per_kv + pp) * LANES
            o_ref[:, lo:lo + LANES] = out.astype(o_ref.dtype)
        yield


def _mixer_tables():
    eye = np.eye(WINDOW, dtype=np.float32)
    eye2 = np.concatenate([eye, eye], axis=1)
    key = np.arange(2 * WINDOW)[:, None]
    qry = np.arange(WINDOW)[None, :] + WINDOW
    band = (key <= qry) & (key > qry - WINDOW)
    first = band & (key >= WINDOW)
    bias = np.stack([np.where(first, 0.0, MASKED), np.where(band, 0.0, MASKED)])
    lane = np.arange(LANES)[:, None]
    out = np.arange(3 * D_SSM)[None, :]
    quantity = (lane % DT_REP) // 2
    expand = ((out // D_SSM == quantity) & ((out % D_SSM) // SSM_HEAD_DIM == lane // DT_REP)
              & (lane % DT_REP < 6))
    return tuple(jnp.asarray(t.astype(np.float32), dtype=jnp.bfloat16)
                 for t in (eye2, bias, expand))


def _layer_call(x, norm_mix_w, w_all, w_t, conv_w, conv_b, dt_bias, a_log, d_skip, ssm_norm_w,
                sinks, w_out, norm_ffn_w, w_gate, w_up, w_down, norm_final_w):
    bsz, seq, _ = x.shape
    nc = seq // CHUNK
    n_groups = (bsz // ROWS_PER_STEP) * nc
    eye2, bias, expand = _mixer_tables()
    group = lambda width, lag: pl.BlockSpec(
        (ROWS_PER_STEP, CHUNK, width),
        lambda s: (jnp.clip(s - lag, 0, n_groups - 1) // nc, jnp.clip(s - lag, 0, n_groups - 1) % nc, 0))
    resident = lambda shape: pl.BlockSpec(shape, lambda s: (0,) * len(shape),
                                          pipeline_mode=pl.Buffered(1))
    gw = HEADS_PER_GROUP * SSM_HEAD_DIM
    bf16 = jnp.bfloat16
    handoff = lambda rows, cols, dtype: pltpu.VMEM((2, ROWS_PER_STEP, rows, cols), dtype)
    return pl.pallas_call(
        functools.partial(_layer_kernel, nc, n_groups),
        grid=(n_groups + PIPELINE_LAG,),
        in_specs=[
            pl.BlockSpec(memory_space=pltpu.SMEM),
            group(D_MODEL, 0), resident((1, D_MODEL)), resident((D_MODEL, D_PROJ)),
            resident((D_PROJ_T, D_MODEL)),
            resident(eye2.shape), resident(bias.shape), resident(expand.shape),
            resident((CONV_WIDTH, D_CONV)), resident((1, D_CONV)), resident((1, LANES)),
            resident((1, LANES)), resident((1, D_SSM)), resident((1, D_SSM)),
            group(D_MODEL, PIPELINE_LAG), resident((D_MIX, D_MODEL)), resident((1, D_MODEL)),
            resident((D_MODEL, D_FF)), resident((D_MODEL, D_FF)), resident((D_FF, D_MODEL)),
            resident((1, D_MODEL)),
        ],
        out_specs=group(D_MODEL, PIPELINE_LAG),
        out_shape=jax.ShapeDtypeStruct((bsz, seq, D_MODEL), jnp.float32),
        scratch_shapes=[
            handoff(CHUNK, D_SSM, bf16), handoff(CHUNK, D_CONV, bf16), handoff(CHUNK, D_KV, bf16),
            handoff(CHUNK, LANES, jnp.float32), handoff(D_ATTN, CHUNK, bf16),
            handoff(D_KV, CHUNK, bf16),
            handoff(CHUNK, D_MIX, bf16),
            pltpu.VMEM((ROWS_PER_STEP, SUBLANES, D_CONV), jnp.float32),
            pltpu.VMEM((ROWS_PER_STEP, 2 * WINDOW, D_KV), bf16),
            pltpu.VMEM((ROWS_PER_STEP, D_KV, 2 * WINDOW), bf16),
            pltpu.VMEM((ROWS_PER_STEP, SSM_GROUPS, D_STATE, gw), jnp.float32),
        ],
        compiler_params=pltpu.CompilerParams(
            dimension_semantics=("arbitrary",), vmem_limit_bytes=VMEM_LIMIT),
        name="layer",
    )(sinks, x, norm_mix_w, w_all, w_t, eye2, bias, expand,
      conv_w, conv_b, dt_bias, a_log, d_skip, ssm_norm_w,
      x, w_out, norm_ffn_w, w_gate, w_up, w_down, norm_final_w)


def _out_ffn_stages(x_ref, ymix_ref, wo_ref, nffn_ref, wg_ref, wu_ref, wd_ref, nfin_ref, o_ref):
    rows = ROWS_PER_STEP * CHUNK
    x1 = x_ref[...].reshape(rows, D_MODEL) + jnp.dot(
        ymix_ref[...].reshape(rows, D_MIX), wo_ref[...], preferred_element_type=jnp.float32)
    h = _rms_norm(x1, nffn_ref[...]).astype(jnp.bfloat16)
    yield
    acts = []
    for lo in range(0, D_FF, FF_SLICE):
        hi = min(lo + FF_SLICE, D_FF)
        gate = jnp.dot(h, wg_ref[:, lo:hi], preferred_element_type=jnp.float32)
        up = jnp.dot(h, wu_ref[:, lo:hi], preferred_element_type=jnp.float32)
        acts.append((_silu(gate) * up).astype(jnp.bfloat16))
        yield
    x2 = x1 + jnp.dot(jnp.concatenate(acts, axis=1), wd_ref[...],
                      preferred_element_type=jnp.float32)
    yield
    o_ref[...] = _rms_norm(x2, nfin_ref[...]).reshape(ROWS_PER_STEP, CHUNK, D_MODEL)


def _per_head_lanes(v):
    return jnp.repeat(v, DT_REP).reshape(1, LANES)


def _layer(x, norm_mix_w, w_in, conv_w, conv_b, dt_bias, a_log, d_skip, ssm_norm_w,
           attn_sinks, w_out, norm_ffn_w, w_gate, w_up, w_down, norm_out_w):
    bsz, seq, _ = x.shape
    bf16 = jnp.bfloat16
    s_xbc = D_SSM + D_CONV
    s_dt = s_xbc + SSM_HEADS
    s_q = s_dt + D_ATTN
    s_k = s_q + D_KV
    w_all = jnp.concatenate(
        [w_in[:, :s_xbc], w_in[:, s_q:s_k], jnp.repeat(w_in[:, s_xbc:s_dt], DT_REP, axis=1)],
        axis=1).astype(bf16)
    w_t = jnp.concatenate(
        [w_in[:, s_dt:s_q] * (HEAD_DIM ** -0.5 * LOG2E), w_in[:, s_k:]], axis=1).T.astype(bf16)
    return _layer_call(
        x, norm_mix_w.reshape(1, D_MODEL), w_all, w_t, conv_w, conv_b.reshape(1, D_CONV),
        _per_head_lanes(dt_bias), _per_head_lanes(a_log),
        jnp.repeat(d_skip, SSM_HEAD_DIM).reshape(1, D_SSM), ssm_norm_w.reshape(1, D_SSM),
        attn_sinks, w_out.astype(bf16), norm_ffn_w.reshape(1, D_MODEL), w_gate.astype(bf16),
        w_up.astype(bf16), w_down.astype(bf16), norm_out_w.reshape(1, D_MODEL))


def kernel(x, norm_mix_w, w_in, conv_w, conv_b, dt_bias, a_log, d_skip, ssm_norm_w,
           attn_sinks, w_out, norm_ffn_w, w_gate, w_up, w_down, norm_final_w):
    depth = norm_mix_w.shape[0]
    assert depth == 1, "final RMSNorm is fused into the single layer's FFN kernel"
    return _layer(x, norm_mix_w[0], w_in[0], conv_w[0], conv_b[0], dt_bias[0], a_log[0],
                  d_skip[0], ssm_norm_w[0], attn_sinks[0], w_out[0], norm_ffn_w[0],
                  w_gate[0], w_up[0], w_down[0], norm_final_w)
```

```python
import functools
import math

import jax
import jax.numpy as jnp
import numpy as np
from jax import lax
from jax.experimental import pallas as pl
from jax.experimental.pallas import tpu as pltpu

D_MODEL = 1024
SSM_HEADS = 8
SSM_HEAD_DIM = 64
D_SSM = SSM_HEADS * SSM_HEAD_DIM
SSM_GROUPS = 2
HEADS_PER_GROUP = SSM_HEADS // SSM_GROUPS
D_STATE = 128
CONV_WIDTH = 4
CHUNK = 128
D_BC = SSM_GROUPS * D_STATE
D_CONV = D_SSM + 2 * D_BC
ATTN_HEADS = 8
KV_HEADS = 2
Q_PER_KV = ATTN_HEADS // KV_HEADS
HEAD_DIM = 64
D_ATTN = ATTN_HEADS * HEAD_DIM
D_KV = KV_HEADS * HEAD_DIM
WINDOW = 128
D_MIX = D_SSM + D_ATTN
D_FF = 2816
EPS = 1e-5

LANES = 128
SUBLANES = 8
HEAD_PAIR = LANES // HEAD_DIM
DT_REP = LANES // SSM_HEADS
LOG2E = math.log2(math.e)
MASKED = -1e30
COL_Z = 0
COL_XBC = COL_Z + D_SSM
COL_K = COL_XBC + D_CONV
COL_DT = COL_K + D_KV
D_PROJ = COL_DT + LANES
D_PROJ_T = D_ATTN + D_KV

PIPELINE_LAG = 2
FF_SLICE = 512
STAGE_ROWS_WIDE = 128
STAGE_ROWS_NARROW = 256
ROWS_PER_STEP = 2
VMEM_LIMIT = 56 * 1024 * 1024

assert CHUNK == WINDOW == D_STATE == LANES and HEAD_DIM == SSM_HEAD_DIM and HEAD_PAIR == 2


def _rms_norm(x, w):
    return x * lax.rsqrt(jnp.mean(x * x, axis=-1, keepdims=True) + EPS) * w


def _silu(x):
    return x * jax.nn.sigmoid(x)


def _bf16_dot(a, b):
    return jnp.dot(a.astype(jnp.bfloat16), b.astype(jnp.bfloat16),
                   preferred_element_type=jnp.float32)


def _bf16_dot_nt(a, b):
    return lax.dot_general(a.astype(jnp.bfloat16), b.astype(jnp.bfloat16),
                           (((1,), (1,)), ((), ())),
                           preferred_element_type=jnp.float32)


def _in_proj_stages(x_ref, nw_ref, w_ref, wt_ref, z_ref, xbc_ref, k_ref, dt_ref, qt_ref, vt_ref):
    rows = ROWS_PER_STEP * CHUNK
    h = _rms_norm(x_ref[...].reshape(rows, D_MODEL), nw_ref[...]).astype(jnp.bfloat16)
    yield
    p = jnp.dot(h, w_ref[...], preferred_element_type=jnp.float32)
    for r in range(ROWS_PER_STEP):
        p_r = p[r * CHUNK:(r + 1) * CHUNK, :]
        z_ref[r] = p_r[:, COL_Z:COL_XBC].astype(z_ref.dtype)
        xbc_ref[r] = p_r[:, COL_XBC:COL_K].astype(xbc_ref.dtype)
        k_ref[r] = p_r[:, COL_K:COL_DT].astype(k_ref.dtype)
        dt_ref[r] = p_r[:, COL_DT:D_PROJ]
    yield
    pt = _bf16_dot_nt(wt_ref[...], h)
    for r in range(ROWS_PER_STEP):
        pt_r = pt[:, r * CHUNK:(r + 1) * CHUNK]
        qt_ref[r] = pt_r[:D_ATTN, :].astype(qt_ref.dtype)
        vt_ref[r] = pt_r[D_ATTN:, :].astype(vt_ref.dtype)


def _split_halves(x, low_half):
    zero = jnp.zeros_like(x)
    return jnp.concatenate([jnp.where(low_half, x, zero), jnp.where(low_half, zero, x)], axis=0)


def _hi_lo(x):
    hi = x.astype(jnp.bfloat16).astype(jnp.float32)
    return hi, x - hi


def _run_interleaved(stage_generators):
    pending = list(stage_generators)
    while pending:
        for gen in list(pending):
            try:
                next(gen)
            except StopIteration:
                pending.remove(gen)


def _stage_bf16(src_hbm, dst_ref, stage_ref, sem):
    slab_rows = stage_ref.shape[1]
    n_slabs = src_hbm.shape[0] // slab_rows
    assert n_slabs * slab_rows == src_hbm.shape[0] and src_hbm.shape[1] == stage_ref.shape[2]

    def slab_copy(i, slot):
        return pltpu.make_async_copy(src_hbm.at[pl.ds(i * slab_rows, slab_rows), :],
                                     stage_ref.at[slot], sem.at[slot])

    slab_copy(0, 0).start()

    def body(i, carry):
        slot = lax.rem(i, 2)

        @pl.when(i + 1 < n_slabs)
        def _():
            slab_copy(i + 1, 1 - slot).start()

        slab_copy(i, slot).wait()
        dst_ref[pl.ds(pl.multiple_of(i * slab_rows, slab_rows), slab_rows), :] = (
            stage_ref[slot].astype(dst_ref.dtype))
        return carry

    lax.fori_loop(0, n_slabs, body, 0)


def _layer_kernel(nc, n_groups, sinks_ref, x_ref, nmix_ref, w_ref, wt_ref,
                  eye_ref, bias_ref, expand_ref,
                  convw_ref, convb_ref, dtb_ref, alog_ref, dskip_ref, nw_ref,
                  xres_ref, wo_hbm, nffn_ref, wg_hbm, wu_hbm, wd_hbm, nfin_ref,
                  o_ref, z_buf, xbc_buf, k_buf, dt_buf, qt_buf, vt_buf, ymix_buf,
                  tail_ref, kext_ref, vtext_ref, state_ref,
                  wo_ref, wg_ref, wu_ref, wd_ref, stage_wide, stage_narrow, stage_sem):
    assert PIPELINE_LAG == 2 and n_groups > 2 * PIPELINE_LAG
    s = pl.program_id(0)
    c = jnp.maximum(lax.rem(s - 1, nc), 0)

    @pl.when(s == 0)
    def _():
        _stage_bf16(wg_hbm, wg_ref, stage_wide, stage_sem)
        _stage_bf16(wu_hbm, wu_ref, stage_wide, stage_sem)
        _stage_bf16(wo_hbm, wo_ref, stage_narrow, stage_sem)
        _stage_bf16(wd_hbm, wd_ref, stage_narrow, stage_sem)

    @pl.when((c == 0) & (s >= 1))
    def _():
        tail_ref[...] = jnp.zeros_like(tail_ref)
        kext_ref[:, 0:WINDOW, :] = jnp.zeros((ROWS_PER_STEP, WINDOW, D_KV), kext_ref.dtype)
        vtext_ref[:, :, 0:WINDOW] = jnp.zeros((ROWS_PER_STEP, D_KV, WINDOW), vtext_ref.dtype)
        state_ref[...] = jnp.zeros_like(state_ref)

    def step(wr, project=True, mix=True, ffn=True):
        rd = 1 - wr
        stages = []
        for r in range(ROWS_PER_STEP if mix else 0):
            stages.append(_attention_stages(
                c, sinks_ref, k_buf.at[rd, r], qt_buf.at[rd, r], vt_buf.at[rd, r], eye_ref,
                bias_ref, ymix_buf.at[wr, r], kext_ref.at[r], vtext_ref.at[r]))
            stages.append(_ssd_stages(
                z_buf.at[rd, r], xbc_buf.at[rd, r], dt_buf.at[rd, r], expand_ref, convw_ref,
                convb_ref, dtb_ref, alog_ref, dskip_ref, nw_ref, ymix_buf.at[wr, r],
                tail_ref.at[r], state_ref.at[r]))
        if project:
            stages.append(_in_proj_stages(
                x_ref, nmix_ref, w_ref, wt_ref, z_buf.at[wr], xbc_buf.at[wr], k_buf.at[wr],
                dt_buf.at[wr], qt_buf.at[wr], vt_buf.at[wr]))
        if ffn:
            stages.append(_out_ffn_stages(xres_ref, ymix_buf.at[rd], wo_ref, nffn_ref, wg_ref,
                                          wu_ref, wd_ref, nfin_ref, o_ref))
        _run_interleaved(stages)

    last = n_groups + PIPELINE_LAG - 1
    pl.when(s == 0)(functools.partial(step, 0, mix=False, ffn=False))
    pl.when(s == 1)(functools.partial(step, 1, ffn=False))
    pl.when(s == last - 1)(functools.partial(step, (last - 1) % 2, project=False))
    pl.when(s == last)(functools.partial(step, last % 2, project=False, mix=False))
    steady = (s >= PIPELINE_LAG) & (s < n_groups)
    for wr in range(2):
        pl.when(steady & (lax.rem(s, 2) == wr))(functools.partial(step, wr))


def _ssd_stages(z_ref, xbc_ref, dt_ref, expand_ref, convw_ref, convb_ref, dtb_ref,
                alog_ref, dskip_ref, nw_ref, o_ref, tail_ref, state_ref):
    cur = xbc_ref[...].astype(jnp.float32)
    tail = tail_ref[...]
    tail_ref[...] = cur[CHUNK - SUBLANES:, :]
    first_rows = lax.broadcasted_iota(jnp.int32, (SUBLANES, D_CONV), 0)
    delayed = [cur]
    for s in range(1, CONV_WIDTH):
        rolled = pltpu.roll(cur, s, 0)
        head = jnp.where(first_rows < s, pltpu.roll(tail, s, 0), rolled[:SUBLANES, :])
        delayed.append(jnp.concatenate([head, rolled[SUBLANES:, :]], axis=0))
    yield

    dt = jax.nn.softplus(dt_ref[...] + dtb_ref[...])
    a_log2 = -jnp.exp(alog_ref[...]) * LOG2E
    a_cum = dt * a_log2
    row = lax.broadcasted_iota(jnp.int32, (CHUNK, LANES), 0)
    shift = 1
    while shift < CHUNK:
        a_cum = a_cum + jnp.where(row >= shift, pltpu.roll(a_cum, shift, 0), 0.0)
        shift *= 2
    a_cum_t = a_cum.T
    a_last = a_cum[CHUNK - 1:CHUNK, :]
    sub = lax.broadcasted_iota(jnp.int32, (CHUNK, LANES), 1) % DT_REP
    packed = jnp.zeros((CHUNK, LANES), jnp.float32)
    for i, val in enumerate((dt, dt * jnp.exp2(a_last - a_cum), jnp.exp2(a_cum))):
        for j, part in enumerate(_hi_lo(val)):
            packed = jnp.where(sub == 2 * i + j, part, packed)
    spread = _bf16_dot(packed, expand_ref[...])
    dt_x = spread[:, :D_SSM]
    dt_end_x = spread[:, D_SSM:2 * D_SSM]
    off_scale = spread[:, 2 * D_SSM:]
    state_scale = off_scale[CHUNK - 1:CHUNK, :]
    yield

    conv = convb_ref[...]
    for s in range(CONV_WIDTH):
        tap = CONV_WIDTH - 1 - s
        conv = conv + convw_ref[tap:tap + 1, :] * delayed[s]
    xbc = _silu(conv)
    xs = xbc[:, :D_SSM]
    b_all = xbc[:, D_SSM:D_SSM + D_BC]
    c_all = xbc[:, D_SSM + D_BC:]
    yield

    xdt = xs * dt_x
    xdt_end = xs * dt_end_x
    gw = HEADS_PER_GROUP * SSM_HEAD_DIM
    cbs, y_offs = [], []
    for g in range(SSM_GROUPS):
        b_g = b_all[:, g * D_STATE:(g + 1) * D_STATE]
        c_g = c_all[:, g * D_STATE:(g + 1) * D_STATE]
        cbs.append(_bf16_dot_nt(c_g, b_g))
        state = state_ref[g]
        y_offs.append(_bf16_dot(c_g, state) * off_scale[:, g * gw:(g + 1) * gw])
        new_states = _bf16_dot(b_g.T, xdt_end[:, g * gw:(g + 1) * gw])
        state_ref[g] = state * state_scale[:, g * gw:(g + 1) * gw] + new_states
    yield

    t_idx = lax.broadcasted_iota(jnp.int32, (CHUNK, CHUNK), 0)
    s_idx = lax.broadcasted_iota(jnp.int32, (CHUNK, CHUNK), 1)
    causal = t_idx >= s_idx
    low_half = s_idx < SSM_HEAD_DIM
    pairs_per_group = HEADS_PER_GROUP // HEAD_PAIR
    y_tiles = []
    for g in range(SSM_GROUPS):
        for pp in range(pairs_per_group):
            pair = g * pairs_per_group + pp
            weights = []
            for u in range(HEAD_PAIR):
                col = (pair * HEAD_PAIR + u) * DT_REP
                seg = a_cum[:, col:col + 1] - a_cum_t[col:col + 1, :]
                weights.append(cbs[g] * jnp.exp2(jnp.where(causal, seg, -jnp.inf)))
            rhs = _split_halves(xdt[:, pair * LANES:(pair + 1) * LANES], low_half)
            y_diag = _bf16_dot(jnp.concatenate(weights, axis=1), rhs)
            y_tiles.append(y_diag + y_offs[g][:, pp * LANES:(pp + 1) * LANES])
        yield

    y = jnp.concatenate(y_tiles, axis=1) + dskip_ref[...] * xs
    y = y * _silu(z_ref[...].astype(jnp.float32))
    for g in range(SSM_GROUPS):
        yg = y[:, g * gw:(g + 1) * gw]
        o_ref[:, g * gw:(g + 1) * gw] = _rms_norm(
            yg, nw_ref[:, g * gw:(g + 1) * gw]).astype(o_ref.dtype)


def _attention_stages(c, sinks_ref, k_ref, qt_ref, vt_ref, eye_ref, bias_ref, o_ref,
                      kext_ref, vtext_ref):
    kext_ref[WINDOW:2 * WINDOW, :] = k_ref[...]
    vtext_ref[:, WINDOW:2 * WINDOW] = vt_ref[...]
    k_ext = kext_ref[...]
    vt_ext = vtext_ref[...]
    kext_ref[0:WINDOW, :] = k_ref[...]
    vtext_ref[:, 0:WINDOW] = vt_ref[...]
    bias_t = bias_ref[jnp.minimum(c, 1)]
    eye2 = eye_ref[...]
    low_kv = lax.broadcasted_iota(jnp.int32, (2 * WINDOW, LANES), 1) < HEAD_DIM
    k_swap = pltpu.roll(k_ext, HEAD_DIM, 1)
    zero_k = jnp.zeros_like(k_ext)
    zero_vt = jnp.zeros((HEAD_DIM, 2 * WINDOW), vt_ext.dtype)
    pairs_per_kv = Q_PER_KV // HEAD_PAIR
    assert pairs_per_kv == 2
    logits, values_t = [], []
    for j in range(KV_HEADS):
        k_lo = jnp.where(low_kv, k_ext if j == 0 else k_swap, zero_k)
        k_hi = jnp.where(low_kv, zero_k, k_swap if j == 0 else k_ext)
        keys = jnp.concatenate([jnp.concatenate([k_lo, bias_t], axis=1),
                                jnp.concatenate([k_hi, bias_t], axis=1)], axis=0)
        rows = slice(j * pairs_per_kv * LANES, (j + 1) * pairs_per_kv * LANES)
        q_t = qt_ref[rows, :]
        queries = jnp.concatenate(
            [jnp.concatenate([q_t[:LANES, :], q_t[LANES:, :]], axis=1), eye2], axis=0)
        logits.append(jnp.dot(keys, queries, preferred_element_type=jnp.float32))
        v_t = vt_ext[j * HEAD_DIM:(j + 1) * HEAD_DIM, :]
        values_t.append(jnp.concatenate([jnp.concatenate([v_t, zero_vt], axis=1),
                                         jnp.concatenate([zero_vt, v_t], axis=1)], axis=0))
    yield

    top_rows = lax.broadcasted_iota(jnp.int32, (LANES, LANES), 0) < HEAD_DIM
    for j in range(KV_HEADS):
        prob_rows, denom_tiles = [], []
        for u in range(HEAD_PAIR):
            prob_tiles, denoms = [], []
            for pp in range(pairs_per_kv):
                s_h = logits[j][u * 2 * WINDOW:(u + 1) * 2 * WINDOW, pp * LANES:(pp + 1) * LANES]
                sink = sinks_ref[(j * pairs_per_kv + pp) * HEAD_PAIR + u] * LOG2E
                m = jnp.maximum(jnp.max(s_h, axis=0, keepdims=True), sink)
                p = jnp.exp2(s_h - m)
                prob_tiles.append(p.astype(jnp.bfloat16))
                denoms.append(jnp.sum(p, axis=0, keepdims=True) + jnp.exp2(sink - m))
            prob_rows.append(jnp.concatenate(prob_tiles, axis=1))
            denom_tiles.append(denoms)
            yield
        probs_t = jnp.concatenate(prob_rows, axis=0)
        out_t = jnp.dot(values_t[j], probs_t, preferred_element_type=jnp.float32)
        for pp in range(pairs_per_kv):
            denom = jnp.where(top_rows, denom_tiles[0][pp], denom_tiles[1][pp])
            out = (out_t[:, pp * LANES:(pp + 1) * LANES] / denom).T
            lo = D_SSM + (j * pairs_per_kv + pp) * LANES
            o_ref[:, lo:lo + LANES] = out.astype(o_ref.dtype)
        yield


def _mixer_tables():
    eye = np.eye(WINDOW, dtype=np.float32)
    eye2 = np.concatenate([eye, eye], axis=1)
    key = np.arange(2 * WINDOW)[:, None]
    qry = np.arange(WINDOW)[None, :] + WINDOW
    band = (key <= qry) & (key > qry - WINDOW)
    first = band & (key >= WINDOW)
    bias = np.stack([np.where(first, 0.0, MASKED), np.where(band, 0.0, MASKED)])
    lane = np.arange(LANES)[:, None]
    out = np.arange(3 * D_SSM)[None, :]
    quantity = (lane % DT_REP) // 2
    expand = ((out // D_SSM == quantity) & ((out % D_SSM) // SSM_HEAD_DIM == lane // DT_REP)
              & (lane % DT_REP < 6))
    return tuple(jnp.asarray(t.astype(np.float32), dtype=jnp.bfloat16)
                 for t in (eye2, bias, expand))


def _layer_call(x, norm_mix_w, w_all, w_t, conv_w, conv_b, dt_bias, a_log, d_skip, ssm_norm_w,
                sinks, w_out, norm_ffn_w, w_gate, w_up, w_down, norm_final_w):
    bsz, seq, _ = x.shape
    nc = seq // CHUNK
    n_groups = (bsz // ROWS_PER_STEP) * nc
    eye2, bias, expand = _mixer_tables()
    group = lambda width, lag: pl.BlockSpec(
        (ROWS_PER_STEP, CHUNK, width),
        lambda s: (jnp.clip(s - lag, 0, n_groups - 1) // nc, jnp.clip(s - lag, 0, n_groups - 1) % nc, 0))
    resident = lambda shape: pl.BlockSpec(shape, lambda s: (0,) * len(shape),
                                          pipeline_mode=pl.Buffered(1))
    gw = HEADS_PER_GROUP * SSM_HEAD_DIM
    bf16 = jnp.bfloat16
    handoff = lambda rows, cols, dtype: pltpu.VMEM((2, ROWS_PER_STEP, rows, cols), dtype)
    in_hbm = pl.BlockSpec(memory_space=pl.ANY)
    return pl.pallas_call(
        functools.partial(_layer_kernel, nc, n_groups),
        grid=(n_groups + PIPELINE_LAG,),
        in_specs=[
            pl.BlockSpec(memory_space=pltpu.SMEM),
            group(D_MODEL, 0), resident((1, D_MODEL)), resident((D_MODEL, D_PROJ)),
            resident((D_PROJ_T, D_MODEL)),
            resident(eye2.shape), resident(bias.shape), resident(expand.shape),
            resident((CONV_WIDTH, D_CONV)), resident((1, D_CONV)), resident((1, LANES)),
            resident((1, LANES)), resident((1, D_SSM)), resident((1, D_SSM)),
            group(D_MODEL, PIPELINE_LAG), in_hbm, resident((1, D_MODEL)),
            in_hbm, in_hbm, in_hbm,
            resident((1, D_MODEL)),
        ],
        out_specs=group(D_MODEL, PIPELINE_LAG),
        out_shape=jax.ShapeDtypeStruct((bsz, seq, D_MODEL), jnp.float32),
        scratch_shapes=[
            handoff(CHUNK, D_SSM, bf16), handoff(CHUNK, D_CONV, bf16), handoff(CHUNK, D_KV, bf16),
            handoff(CHUNK, LANES, jnp.float32), handoff(D_ATTN, CHUNK, bf16),
            handoff(D_KV, CHUNK, bf16),
            handoff(CHUNK, D_MIX, bf16),
            pltpu.VMEM((ROWS_PER_STEP, SUBLANES, D_CONV), jnp.float32),
            pltpu.VMEM((ROWS_PER_STEP, 2 * WINDOW, D_KV), bf16),
            pltpu.VMEM((ROWS_PER_STEP, D_KV, 2 * WINDOW), bf16),
            pltpu.VMEM((ROWS_PER_STEP, SSM_GROUPS, D_STATE, gw), jnp.float32),
            pltpu.VMEM((D_MIX, D_MODEL), bf16), pltpu.VMEM((D_MODEL, D_FF), bf16),
            pltpu.VMEM((D_MODEL, D_FF), bf16), pltpu.VMEM((D_FF, D_MODEL), bf16),
            pltpu.VMEM((2, STAGE_ROWS_WIDE, D_FF), jnp.float32),
            pltpu.VMEM((2, STAGE_ROWS_NARROW, D_MODEL), jnp.float32),
            pltpu.SemaphoreType.DMA((2,)),
        ],
        compiler_params=pltpu.CompilerParams(
            dimension_semantics=("arbitrary",), vmem_limit_bytes=VMEM_LIMIT),
        name="layer",
    )(sinks, x, norm_mix_w, w_all, w_t, eye2, bias, expand,
      conv_w, conv_b, dt_bias, a_log, d_skip, ssm_norm_w,
      x, w_out, norm_ffn_w, w_gate, w_up, w_down, norm_final_w)


def _out_ffn_stages(x_ref, ymix_ref, wo_ref, nffn_ref, wg_ref, wu_ref, wd_ref, nfin_ref, o_ref):
    rows = ROWS_PER_STEP * CHUNK
    x1 = x_ref[...].reshape(rows, D_MODEL) + jnp.dot(
        ymix_ref[...].reshape(rows, D_MIX), wo_ref[...], preferred_element_type=jnp.float32)
    h = _rms_norm(x1, nffn_ref[...]).astype(jnp.bfloat16)
    yield
    acts = []
    for lo in range(0, D_FF, FF_SLICE):
        hi = min(lo + FF_SLICE, D_FF)
        gate = jnp.dot(h, wg_ref[:, lo:hi], preferred_element_type=jnp.float32)
        up = jnp.dot(h, wu_ref[:, lo:hi], preferred_element_type=jnp.float32)
        acts.append((_silu(gate) * up).astype(jnp.bfloat16))
        yield
    x2 = x1 + jnp.dot(jnp.concatenate(acts, axis=1), wd_ref[...],
                      preferred_element_type=jnp.float32)
    yield
    o_ref[...] = _rms_norm(x2, nfin_ref[...]).reshape(ROWS_PER_STEP, CHUNK, D_MODEL)


def _per_head_lanes(v):
    return jnp.repeat(v, DT_REP).reshape(1, LANES)


def _layer(x, norm_mix_w, w_in, conv_w, conv_b, dt_bias, a_log, d_skip, ssm_norm_w,
           attn_sinks, w_out, norm_ffn_w, w_gate, w_up, w_down, norm_out_w):
    bsz, seq, _ = x.shape
    bf16 = jnp.bfloat16
    s_xbc = D_SSM + D_CONV
    s_dt = s_xbc + SSM_HEADS
    s_q = s_dt + D_ATTN
    s_k = s_q + D_KV
    w_all = jnp.concatenate(
        [w_in[:, :s_xbc], w_in[:, s_q:s_k], jnp.repeat(w_in[:, s_xbc:s_dt], DT_REP, axis=1)],
        axis=1).astype(bf16)
    w_t = jnp.concatenate(
        [w_in[:, s_dt:s_q] * (HEAD_DIM ** -0.5 * LOG2E), w_in[:, s_k:]], axis=1).T.astype(bf16)
    return _layer_call(
        x, norm_mix_w.reshape(1, D_MODEL), w_all, w_t, conv_w, conv_b.reshape(1, D_CONV),
        _per_head_lanes(dt_bias), _per_head_lanes(a_log),
        jnp.repeat(d_skip, SSM_HEAD_DIM).reshape(1, D_SSM), ssm_norm_w.reshape(1, D_SSM),
        attn_sinks, w_out, norm_ffn_w.reshape(1, D_MODEL), w_gate, w_up, w_down,
        norm_out_w.reshape(1, D_MODEL))


def kernel(x, norm_mix_w, w_in, conv_w, conv_b, dt_bias, a_log, d_skip, ssm_norm_w,
           attn_sinks, w_out, norm_ffn_w, w_gate, w_up, w_down, norm_final_w):
    depth = norm_mix_w.shape[0]
    assert depth == 1, "final RMSNorm is fused into the single layer's FFN kernel"
    return _layer(x, norm_mix_w[0], w_in[0], conv_w[0], conv_b[0], dt_bias[0], a_log[0],
                  d_skip[0], ssm_norm_w[0], attn_sinks[0], w_out[0], norm_ffn_w[0],
                  w_gate[0], w_up[0], w_down[0], norm_final_w)
```

```python
import functools
import math

import jax
import jax.numpy as jnp
import numpy as np
from jax import lax
from jax.experimental import pallas as pl
from jax.experimental.pallas import tpu as pltpu

D_MODEL = 1024
SSM_HEADS = 8
SSM_HEAD_DIM = 64
D_SSM = SSM_HEADS * SSM_HEAD_DIM
SSM_GROUPS = 2
HEADS_PER_GROUP = SSM_HEADS // SSM_GROUPS
D_STATE = 128
CONV_WIDTH = 4
CHUNK = 128
D_BC = SSM_GROUPS * D_STATE
D_CONV = D_SSM + 2 * D_BC
ATTN_HEADS = 8
KV_HEADS = 2
Q_PER_KV = ATTN_HEADS // KV_HEADS
HEAD_DIM = 64
D_ATTN = ATTN_HEADS * HEAD_DIM
D_KV = KV_HEADS * HEAD_DIM
WINDOW = 128
D_MIX = D_SSM + D_ATTN
D_FF = 2816
EPS = 1e-5

LANES = 128
SUBLANES = 8
HEAD_PAIR = LANES // HEAD_DIM
DT_REP = LANES // SSM_HEADS
LOG2E = math.log2(math.e)
MASKED = -1e30
COL_Z = 0
COL_XBC = COL_Z + D_SSM
COL_K = COL_XBC + D_CONV
COL_DT = COL_K + D_KV
D_PROJ = COL_DT + LANES
D_PROJ_T = D_ATTN + D_KV

PIPELINE_LAG = 2
FF_SLICE = 512
(ROW_NORM_MIX, ROW_NORM_FFN, ROW_NORM_OUT, ROW_CONV_B, ROW_SSM, ROW_DT,
 ROW_CONV_W) = range(0, 7 * SUBLANES, SUBLANES)
VEC_ROWS = 7 * SUBLANES
STAGE_ROWS_WIDE = 128
STAGE_ROWS_NARROW = 256
ROWS_PER_STEP = 2
VMEM_LIMIT = 56 * 1024 * 1024

assert CHUNK == WINDOW == D_STATE == LANES and HEAD_DIM == SSM_HEAD_DIM and HEAD_PAIR == 2
assert D_CONV == D_MODEL == 2 * D_SSM


def _rms_norm(x, w):
    return x * lax.rsqrt(jnp.mean(x * x, axis=-1, keepdims=True) + EPS) * w


def _silu(x):
    return x * jax.nn.sigmoid(x)


def _bf16_dot(a, b):
    return jnp.dot(a.astype(jnp.bfloat16), b.astype(jnp.bfloat16),
                   preferred_element_type=jnp.float32)


def _bf16_dot_nt(a, b):
    return lax.dot_general(a.astype(jnp.bfloat16), b.astype(jnp.bfloat16),
                           (((1,), (1,)), ((), ())),
                           preferred_element_type=jnp.float32)


def _in_proj_stages(x_ref, nw_ref, w_ref, wt_ref, z_ref, xbc_ref, k_ref, dt_ref, qt_ref, vt_ref):
    rows = ROWS_PER_STEP * CHUNK
    h = _rms_norm(x_ref[...].reshape(rows, D_MODEL), nw_ref[...]).astype(jnp.bfloat16)
    yield
    p = jnp.dot(h, w_ref[...], preferred_element_type=jnp.float32)
    for r in range(ROWS_PER_STEP):
        p_r = p[r * CHUNK:(r + 1) * CHUNK, :]
        z_ref[r] = p_r[:, COL_Z:COL_XBC].astype(z_ref.dtype)
        xbc_ref[r] = p_r[:, COL_XBC:COL_K].astype(xbc_ref.dtype)
        k_ref[r] = p_r[:, COL_K:COL_DT].astype(k_ref.dtype)
        dt_ref[r] = p_r[:, COL_DT:D_PROJ]
    yield
    pt = _bf16_dot_nt(wt_ref[...], h)
    for r in range(ROWS_PER_STEP):
        pt_r = pt[:, r * CHUNK:(r + 1) * CHUNK]
        qt_ref[r] = pt_r[:D_ATTN, :].astype(qt_ref.dtype)
        vt_ref[r] = pt_r[D_ATTN:, :].astype(vt_ref.dtype)


def _split_halves(x, low_half):
    zero = jnp.zeros_like(x)
    return jnp.concatenate([jnp.where(low_half, x, zero), jnp.where(low_half, zero, x)], axis=0)


def _hi_lo(x):
    hi = x.astype(jnp.bfloat16).astype(jnp.float32)
    return hi, x - hi


def _run_interleaved(stage_generators):
    pending = list(stage_generators)
    while pending:
        for gen in list(pending):
            try:
                next(gen)
            except StopIteration:
                pending.remove(gen)


def _stage_bf16(jobs):
    plans = []
    for matrices, stage_ref, sem in jobs:
        slab_rows = stage_ref.shape[1]
        slabs = []
        for src_hbm, dst_ref in matrices:
            assert src_hbm.shape[0] % slab_rows == 0 and src_hbm.shape[1] == stage_ref.shape[2]
            slabs += [(src_hbm, dst_ref, lo) for lo in range(0, src_hbm.shape[0], slab_rows)]
        copies = [pltpu.make_async_copy(src.at[lo:lo + slab_rows, :], stage_ref.at[k % 2],
                                        sem.at[k % 2])
                  for k, (src, _, lo) in enumerate(slabs)]
        plans.append((slabs, copies, stage_ref, slab_rows))
    for _, copies, _, _ in plans:
        copies[0].start()
    for k in range(max(len(copies) for _, copies, _, _ in plans)):
        for slabs, copies, stage_ref, slab_rows in plans:
            if k < len(copies):
                if k + 1 < len(copies):
                    copies[k + 1].start()
                copies[k].wait()
                _, dst_ref, lo = slabs[k]
                dst_ref[lo:lo + slab_rows, :] = stage_ref[k % 2].astype(dst_ref.dtype)


def _layer_kernel(nc, n_groups, sinks_ref, x_ref, vec_ref, w_ref, wt_ref,
                  eye_ref, bias_ref, expand_ref,
                  xres_ref, wo_hbm, wg_hbm, wu_hbm, wd_hbm,
                  o_ref, z_buf, xbc_buf, k_buf, dt_buf, qt_buf, vt_buf, ymix_buf,
                  tail_ref, kext_ref, vtext_ref, state_ref,
                  wo_ref, wg_ref, wu_ref, wd_ref, stage_wide, stage_narrow, sem_wide, sem_narrow):
    assert PIPELINE_LAG == 2 and n_groups > 2 * PIPELINE_LAG
    s = pl.program_id(0)
    c = jnp.maximum(lax.rem(s - 1, nc), 0)
    row = lambda r, lo=0, hi=D_MODEL, n=1: vec_ref.at[r:r + n, lo:hi]
    nmix_ref, nffn_ref, nfin_ref = row(ROW_NORM_MIX), row(ROW_NORM_FFN), row(ROW_NORM_OUT)
    convb_ref, convw_ref = row(ROW_CONV_B), row(ROW_CONV_W, n=CONV_WIDTH)
    nw_ref, dskip_ref = row(ROW_SSM, 0, D_SSM), row(ROW_SSM, D_SSM, 2 * D_SSM)
    dtb_ref, alog_ref = row(ROW_DT, 0, LANES), row(ROW_DT, LANES, 2 * LANES)

    @pl.when(s == 0)
    def _():
        _stage_bf16([([(wg_hbm, wg_ref), (wu_hbm, wu_ref)], stage_wide, sem_wide),
                     ([(wo_hbm, wo_ref), (wd_hbm, wd_ref)], stage_narrow, sem_narrow)])

    @pl.when((c == 0) & (s >= 1))
    def _():
        tail_ref[...] = jnp.zeros_like(tail_ref)
        kext_ref[:, 0:WINDOW, :] = jnp.zeros((ROWS_PER_STEP, WINDOW, D_KV), kext_ref.dtype)
        vtext_ref[:, :, 0:WINDOW] = jnp.zeros((ROWS_PER_STEP, D_KV, WINDOW), vtext_ref.dtype)
        state_ref[...] = jnp.zeros_like(state_ref)

    def step(wr, project=True, mix=True, ffn=True):
        rd = 1 - wr
        stages = []
        for r in range(ROWS_PER_STEP if mix else 0):
            stages.append(_attention_stages(
                c, sinks_ref, k_buf.at[rd, r], qt_buf.at[rd, r], vt_buf.at[rd, r], eye_ref,
                bias_ref, ymix_buf.at[wr, r], kext_ref.at[r], vtext_ref.at[r]))
            stages.append(_ssd_stages(
                z_buf.at[rd, r], xbc_buf.at[rd, r], dt_buf.at[rd, r], expand_ref, convw_ref,
                convb_ref, dtb_ref, alog_ref, dskip_ref, nw_ref, ymix_buf.at[wr, r],
                tail_ref.at[r], state_ref.at[r]))
        if project:
            stages.append(_in_proj_stages(
                x_ref, nmix_ref, w_ref, wt_ref, z_buf.at[wr], xbc_buf.at[wr], k_buf.at[wr],
                dt_buf.at[wr], qt_buf.at[wr], vt_buf.at[wr]))
        if ffn:
            stages.append(_out_ffn_stages(xres_ref, ymix_buf.at[rd], wo_ref, nffn_ref, wg_ref,
                                          wu_ref, wd_ref, nfin_ref, o_ref))
        _run_interleaved(stages)

    last = n_groups + PIPELINE_LAG - 1
    pl.when(s == 0)(functools.partial(step, 0, mix=False, ffn=False))
    pl.when(s == 1)(functools.partial(step, 1, ffn=False))
    pl.when(s == last - 1)(functools.partial(step, (last - 1) % 2, project=False))
    pl.when(s == last)(functools.partial(step, last % 2, project=False, mix=False))
    steady = (s >= PIPELINE_LAG) & (s < n_groups)
    for wr in range(2):
        pl.when(steady & (lax.rem(s, 2) == wr))(functools.partial(step, wr))


def _ssd_stages(z_ref, xbc_ref, dt_ref, expand_ref, convw_ref, convb_ref, dtb_ref,
                alog_ref, dskip_ref, nw_ref, o_ref, tail_ref, state_ref):
    cur = xbc_ref[...].astype(jnp.float32)
    tail = tail_ref[...]
    tail_ref[...] = cur[CHUNK - SUBLANES:, :]
    first_rows = lax.broadcasted_iota(jnp.int32, (SUBLANES, D_CONV), 0)
    delayed = [cur]
    for s in range(1, CONV_WIDTH):
        rolled = pltpu.roll(cur, s, 0)
        head = jnp.where(first_rows < s, pltpu.roll(tail, s, 0), rolled[:SUBLANES, :])
        delayed.append(jnp.concatenate([head, rolled[SUBLANES:, :]], axis=0))
    yield

    dt = jax.nn.softplus(dt_ref[...] + dtb_ref[...])
    a_log2 = -jnp.exp(alog_ref[...]) * LOG2E
    a_cum = dt * a_log2
    row = lax.broadcasted_iota(jnp.int32, (CHUNK, LANES), 0)
    shift = 1
    while shift < CHUNK:
        a_cum = a_cum + jnp.where(row >= shift, pltpu.roll(a_cum, shift, 0), 0.0)
        shift *= 2
    a_cum_t = a_cum.T
    a_last = a_cum[CHUNK - 1:CHUNK, :]
    sub = lax.broadcasted_iota(jnp.int32, (CHUNK, LANES), 1) % DT_REP
    packed = jnp.zeros((CHUNK, LANES), jnp.float32)
    for i, val in enumerate((dt, dt * jnp.exp2(a_last - a_cum), jnp.exp2(a_cum))):
        for j, part in enumerate(_hi_lo(val)):
            packed = jnp.where(sub == 2 * i + j, part, packed)
    spread = _bf16_dot(packed, expand_ref[...])
    dt_x = spread[:, :D_SSM]
    dt_end_x = spread[:, D_SSM:2 * D_SSM]
    off_scale = spread[:, 2 * D_SSM:]
    state_scale = off_scale[CHUNK - 1:CHUNK, :]
    yield

    conv = convb_ref[...]
    for s in range(CONV_WIDTH):
        tap = CONV_WIDTH - 1 - s
        conv = conv + convw_ref[tap:tap + 1, :] * delayed[s]
    xbc = _silu(conv)
    xs = xbc[:, :D_SSM]
    b_all = xbc[:, D_SSM:D_SSM + D_BC]
    c_all = xbc[:, D_SSM + D_BC:]
    yield

    xdt = xs * dt_x
    xdt_end = xs * dt_end_x
    gw = HEADS_PER_GROUP * SSM_HEAD_DIM
    cbs, y_offs = [], []
    for g in range(SSM_GROUPS):
        b_g = b_all[:, g * D_STATE:(g + 1) * D_STATE]
        c_g = c_all[:, g * D_STATE:(g + 1) * D_STATE]
        cbs.append(_bf16_dot_nt(c_g, b_g))
        state = state_ref[g]
        y_offs.append(_bf16_dot(c_g, state) * off_scale[:, g * gw:(g + 1) * gw])
        new_states = _bf16_dot(b_g.T, xdt_end[:, g * gw:(g + 1) * gw])
        state_ref[g] = state * state_scale[:, g * gw:(g + 1) * gw] + new_states
    yield

    t_idx = lax.broadcasted_iota(jnp.int32, (CHUNK, CHUNK), 0)
    s_idx = lax.broadcasted_iota(jnp.int32, (CHUNK, CHUNK), 1)
    causal = t_idx >= s_idx
    low_half = s_idx < SSM_HEAD_DIM
    pairs_per_group = HEADS_PER_GROUP // HEAD_PAIR
    y_tiles = []
    for g in range(SSM_GROUPS):
        for pp in range(pairs_per_group):
            pair = g * pairs_per_group + pp
            weights = []
            for u in range(HEAD_PAIR):
                col = (pair * HEAD_PAIR + u) * DT_REP
                seg = a_cum[:, col:col + 1] - a_cum_t[col:col + 1, :]
                weights.append(cbs[g] * jnp.exp2(jnp.where(causal, seg, -jnp.inf)))
            rhs = _split_halves(xdt[:, pair * LANES:(pair + 1) * LANES], low_half)
            y_diag = _bf16_dot(jnp.concatenate(weights, axis=1), rhs)
            y_tiles.append(y_diag + y_offs[g][:, pp * LANES:(pp + 1) * LANES])
        yield

    y = jnp.concatenate(y_tiles, axis=1) + dskip_ref[...] * xs
    y = y * _silu(z_ref[...].astype(jnp.float32))
    for g in range(SSM_GROUPS):
        yg = y[:, g * gw:(g + 1) * gw]
        o_ref[:, g * gw:(g + 1) * gw] = _rms_norm(
            yg, nw_ref[:, g * gw:(g + 1) * gw]).astype(o_ref.dtype)


def _attention_stages(c, sinks_ref, k_ref, qt_ref, vt_ref, eye_ref, bias_ref, o_ref,
                      kext_ref, vtext_ref):
    kext_ref[WINDOW:2 * WINDOW, :] = k_ref[...]
    vtext_ref[:, WINDOW:2 * WINDOW] = vt_ref[...]
    k_ext = kext_ref[...]
    vt_ext = vtext_ref[...]
    kext_ref[0:WINDOW, :] = k_ref[...]
    vtext_ref[:, 0:WINDOW] = vt_ref[...]
    bias_t = bias_ref[jnp.minimum(c, 1)]
    eye2 = eye_ref[...]
    low_kv = lax.broadcasted_iota(jnp.int32, (2 * WINDOW, LANES), 1) < HEAD_DIM
    k_swap = pltpu.roll(k_ext, HEAD_DIM, 1)
    zero_k = jnp.zeros_like(k_ext)
    zero_vt = jnp.zeros((HEAD_DIM, 2 * WINDOW), vt_ext.dtype)
    pairs_per_kv = Q_PER_KV // HEAD_PAIR
    assert pairs_per_kv == 2
    logits, values_t = [], []
    for j in range(KV_HEADS):
        k_lo = jnp.where(low_kv, k_ext if j == 0 else k_swap, zero_k)
        k_hi = jnp.where(low_kv, zero_k, k_swap if j == 0 else k_ext)
        keys = jnp.concatenate([jnp.concatenate([k_lo, bias_t], axis=1),
                                jnp.concatenate([k_hi, bias_t], axis=1)], axis=0)
        rows = slice(j * pairs_per_kv * LANES, (j + 1) * pairs_per_kv * LANES)
        q_t = qt_ref[rows, :]
        queries = jnp.concatenate(
            [jnp.concatenate([q_t[:LANES, :], q_t[LANES:, :]], axis=1), eye2], axis=0)
        logits.append(jnp.dot(keys, queries, preferred_element_type=jnp.float32))
        v_t = vt_ext[j * HEAD_DIM:(j + 1) * HEAD_DIM, :]
        values_t.append(jnp.concatenate([jnp.concatenate([v_t, zero_vt], axis=1),
                                         jnp.concatenate([zero_vt, v_t], axis=1)], axis=0))
    yield

    top_rows = lax.broadcasted_iota(jnp.int32, (LANES, LANES), 0) < HEAD_DIM
    for j in range(KV_HEADS):
        prob_rows, denom_tiles = [], []
        for u in range(HEAD_PAIR):
            prob_tiles, denoms = [], []
            for pp in range(pairs_per_kv):
                s_h = logits[j][u * 2 * WINDOW:(u + 1) * 2 * WINDOW, pp * LANES:(pp + 1) * LANES]
                sink = sinks_ref[(j * pairs_per_kv + pp) * HEAD_PAIR + u] * LOG2E
                m = jnp.maximum(jnp.max(s_h, axis=0, keepdims=True), sink)
                p = jnp.exp2(s_h - m)
                prob_tiles.append(p.astype(jnp.bfloat16))
                denoms.append(jnp.sum(p, axis=0, keepdims=True) + jnp.exp2(sink - m))
            prob_rows.append(jnp.concatenate(prob_tiles, axis=1))
            denom_tiles.append(denoms)
            yield
        probs_t = jnp.concatenate(prob_rows, axis=0)
        out_t = jnp.dot(values_t[j], probs_t, preferred_element_type=jnp.float32)
        for pp in range(pairs_per_kv):
            denom = jnp.where(top_rows, denom_tiles[0][pp], denom_tiles[1][pp])
            out = (out_t[:, pp * LANES:(pp + 1) * LANES] / denom).T
            lo = D_SSM + (j * pairs_per_kv + pp) * LANES
            o_ref[:, lo:lo + LANES] = out.astype(o_ref.dtype)
        yield


def _mixer_tables():
    eye = np.eye(WINDOW, dtype=np.float32)
    eye2 = np.concatenate([eye, eye], axis=1)
    key = np.arange(2 * WINDOW)[:, None]
    qry = np.arange(WINDOW)[None, :] + WINDOW
    band = (key <= qry) & (key > qry - WINDOW)
    first = band & (key >= WINDOW)
    bias = np.stack([np.where(first, 0.0, MASKED), np.where(band, 0.0, MASKED)])
    lane = np.arange(LANES)[:, None]
    out = np.arange(3 * D_SSM)[None, :]
    quantity = (lane % DT_REP) // 2
    expand = ((out // D_SSM == quantity) & ((out % D_SSM) // SSM_HEAD_DIM == lane // DT_REP)
              & (lane % DT_REP < 6))
    return tuple(jnp.asarray(t.astype(np.float32), dtype=jnp.bfloat16)
                 for t in (eye2, bias, expand))


def _layer_call(x, vec, w_all, w_t, sinks, w_out, w_gate, w_up, w_down):
    bsz, seq, _ = x.shape
    nc = seq // CHUNK
    n_groups = (bsz // ROWS_PER_STEP) * nc
    eye2, bias, expand = _mixer_tables()
    group = lambda width, lag: pl.BlockSpec(
        (ROWS_PER_STEP, CHUNK, width),
        lambda s: (jnp.clip(s - lag, 0, n_groups - 1) // nc, jnp.clip(s - lag, 0, n_groups - 1) % nc, 0))
    resident = lambda shape: pl.BlockSpec(shape, lambda s: (0,) * len(shape),
                                          pipeline_mode=pl.Buffered(1))
    gw = HEADS_PER_GROUP * SSM_HEAD_DIM
    bf16 = jnp.bfloat16
    handoff = lambda rows, cols, dtype: pltpu.VMEM((2, ROWS_PER_STEP, rows, cols), dtype)
    in_hbm = pl.BlockSpec(memory_space=pl.ANY)
    return pl.pallas_call(
        functools.partial(_layer_kernel, nc, n_groups),
        grid=(n_groups + PIPELINE_LAG,),
        in_specs=[
            pl.BlockSpec(memory_space=pltpu.SMEM),
            group(D_MODEL, 0), resident((VEC_ROWS, D_MODEL)), resident((D_MODEL, D_PROJ)),
            resident((D_PROJ_T, D_MODEL)),
            resident(eye2.shape), resident(bias.shape), resident(expand.shape),
            group(D_MODEL, PIPELINE_LAG), in_hbm, in_hbm, in_hbm, in_hbm,
        ],
        out_specs=group(D_MODEL, PIPELINE_LAG),
        out_shape=jax.ShapeDtypeStruct((bsz, seq, D_MODEL), jnp.float32),
        scratch_shapes=[
            handoff(CHUNK, D_SSM, bf16), handoff(CHUNK, D_CONV, bf16), handoff(CHUNK, D_KV, bf16),
            handoff(CHUNK, LANES, jnp.float32), handoff(D_ATTN, CHUNK, bf16),
            handoff(D_KV, CHUNK, bf16),
            handoff(CHUNK, D_MIX, bf16),
            pltpu.VMEM((ROWS_PER_STEP, SUBLANES, D_CONV), jnp.float32),
            pltpu.VMEM((ROWS_PER_STEP, 2 * WINDOW, D_KV), bf16),
            pltpu.VMEM((ROWS_PER_STEP, D_KV, 2 * WINDOW), bf16),
            pltpu.VMEM((ROWS_PER_STEP, SSM_GROUPS, D_STATE, gw), jnp.float32),
            pltpu.VMEM((D_MIX, D_MODEL), bf16), pltpu.VMEM((D_MODEL, D_FF), bf16),
            pltpu.VMEM((D_MODEL, D_FF), bf16), pltpu.VMEM((D_FF, D_MODEL), bf16),
            pltpu.VMEM((2, STAGE_ROWS_WIDE, D_FF), jnp.float32),
            pltpu.VMEM((2, STAGE_ROWS_NARROW, D_MODEL), jnp.float32),
            pltpu.SemaphoreType.DMA((2,)), pltpu.SemaphoreType.DMA((2,)),
        ],
        compiler_params=pltpu.CompilerParams(
            dimension_semantics=("arbitrary",), vmem_limit_bytes=VMEM_LIMIT),
        name="layer",
    )(sinks, x, vec, w_all, w_t, eye2, bias, expand, x, w_out, w_gate, w_up, w_down)


def _out_ffn_stages(x_ref, ymix_ref, wo_ref, nffn_ref, wg_ref, wu_ref, wd_ref, nfin_ref, o_ref):
    rows = ROWS_PER_STEP * CHUNK
    x1 = x_ref[...].reshape(rows, D_MODEL) + jnp.dot(
        ymix_ref[...].reshape(rows, D_MIX), wo_ref[...], preferred_element_type=jnp.float32)
    h = _rms_norm(x1, nffn_ref[...]).astype(jnp.bfloat16)
    yield
    acts = []
    for lo in range(0, D_FF, FF_SLICE):
        hi = min(lo + FF_SLICE, D_FF)
        gate = jnp.dot(h, wg_ref[:, lo:hi], preferred_element_type=jnp.float32)
        up = jnp.dot(h, wu_ref[:, lo:hi], preferred_element_type=jnp.float32)
        acts.append((_silu(gate) * up).astype(jnp.bfloat16))
        yield
    x2 = x1 + jnp.dot(jnp.concatenate(acts, axis=1), wd_ref[...],
                      preferred_element_type=jnp.float32)
    yield
    o_ref[...] = _rms_norm(x2, nfin_ref[...]).reshape(ROWS_PER_STEP, CHUNK, D_MODEL)


def _layer(x, norm_mix_w, w_in, conv_w, conv_b, dt_bias, a_log, d_skip, ssm_norm_w,
           attn_sinks, w_out, norm_ffn_w, w_gate, w_up, w_down, norm_out_w):
    bsz, seq, _ = x.shape
    bf16 = jnp.bfloat16
    s_xbc = D_SSM + D_CONV
    s_dt = s_xbc + SSM_HEADS
    s_q = s_dt + D_ATTN
    s_k = s_q + D_KV
    w_all = jnp.concatenate(
        [w_in[:, :s_xbc], w_in[:, s_q:s_k], jnp.repeat(w_in[:, s_xbc:s_dt], DT_REP, axis=1)],
        axis=1).astype(bf16)
    w_t = jnp.concatenate(
        [w_in[:, s_dt:s_q] * (HEAD_DIM ** -0.5 * LOG2E), w_in[:, s_k:]], axis=1).T.astype(bf16)
    per_head = jnp.concatenate([jnp.repeat(dt_bias, DT_REP), jnp.repeat(a_log, DT_REP)])
    vectors = jnp.stack([norm_mix_w, norm_ffn_w, norm_out_w, conv_b,
                         jnp.concatenate([ssm_norm_w, jnp.repeat(d_skip, SSM_HEAD_DIM)]),
                         jnp.pad(per_head, (0, D_MODEL - 2 * LANES))])
    vec = jnp.concatenate([
        jnp.pad(vectors[:, None, :], ((0, 0), (0, SUBLANES - 1), (0, 0))).reshape(-1, D_MODEL),
        jnp.pad(conv_w, ((0, SUBLANES - CONV_WIDTH), (0, 0)))])
    return _layer_call(x, vec, w_all, w_t, attn_sinks, w_out, w_gate, w_up, w_down)


def kernel(x, norm_mix_w, w_in, conv_w, conv_b, dt_bias, a_log, d_skip, ssm_norm_w,
           attn_sinks, w_out, norm_ffn_w, w_gate, w_up, w_down, norm_final_w):
    depth = norm_mix_w.shape[0]
    assert depth == 1, "final RMSNorm is fused into the single layer's FFN kernel"
    return _layer(x, norm_mix_w[0], w_in[0], conv_w[0], conv_b[0], dt_bias[0], a_log[0],
                  d_skip[0], ssm_norm_w[0], attn_sinks[0], w_out[0], norm_ffn_w[0],
                  w_gate[0], w_up[0], w_down[0], norm_final_w)
```

```python
import functools
import math

import jax
import jax.numpy as jnp
import numpy as np
from jax import lax
from jax.experimental import pallas as pl
from jax.experimental.pallas import tpu as pltpu

D_MODEL = 1024
SSM_HEADS = 8
SSM_HEAD_DIM = 64
D_SSM = SSM_HEADS * SSM_HEAD_DIM
SSM_GROUPS = 2
HEADS_PER_GROUP = SSM_HEADS // SSM_GROUPS
D_STATE = 128
CONV_WIDTH = 4
CHUNK = 128
D_BC = SSM_GROUPS * D_STATE
D_CONV = D_SSM + 2 * D_BC
ATTN_HEADS = 8
KV_HEADS = 2
Q_PER_KV = ATTN_HEADS // KV_HEADS
HEAD_DIM = 64
D_ATTN = ATTN_HEADS * HEAD_DIM
D_KV = KV_HEADS * HEAD_DIM
WINDOW = 128
D_MIX = D_SSM + D_ATTN
D_FF = 2816
EPS = 1e-5

LANES = 128
SUBLANES = 8
HEAD_PAIR = LANES // HEAD_DIM
DT_REP = LANES // SSM_HEADS
LOG2E = math.log2(math.e)
MASKED = -1e30
COL_Z = 0
COL_XBC = COL_Z + D_SSM
COL_K = COL_XBC + D_CONV
COL_DT = COL_K + D_KV
D_PROJ = COL_DT + LANES
D_PROJ_T = D_ATTN + D_KV

PIPELINE_LAG = 2
FF_SLICE = 256
(ROW_NORM_MIX, ROW_NORM_FFN, ROW_NORM_OUT, ROW_CONV_B, ROW_SSM, ROW_DT,
 ROW_CONV_W) = range(0, 7 * SUBLANES, SUBLANES)
VEC_ROWS = 7 * SUBLANES
STAGE_ROWS_WIDE = 128
STAGE_ROWS_NARROW = 256
ROWS_PER_STEP = 2
V7X_VMEM_BYTES = 64 * 2 ** 20
VMEM_UNREQUESTED = 8 * 2 ** 20
VMEM_LIMIT = V7X_VMEM_BYTES - VMEM_UNREQUESTED

assert CHUNK == WINDOW == D_STATE == LANES and HEAD_DIM == SSM_HEAD_DIM and HEAD_PAIR == 2
assert D_CONV == D_MODEL == 2 * D_SSM


def _rms_norm(x, w):
    return x * lax.rsqrt(jnp.mean(x * x, axis=-1, keepdims=True) + EPS) * w


def _silu(x):
    return x * jax.nn.sigmoid(x)


def _bf16_dot(a, b):
    return jnp.dot(a.astype(jnp.bfloat16), b.astype(jnp.bfloat16),
                   preferred_element_type=jnp.float32)


def _bf16_dot_nt(a, b):
    return lax.dot_general(a.astype(jnp.bfloat16), b.astype(jnp.bfloat16),
                           (((1,), (1,)), ((), ())),
                           preferred_element_type=jnp.float32)


def _in_proj_stages(x_ref, nw_ref, w_ref, wt_ref, z_ref, xbc_ref, k_ref, dt_ref, qt_ref, vt_ref):
    rows = ROWS_PER_STEP * CHUNK
    h = _rms_norm(x_ref[...].reshape(rows, D_MODEL), nw_ref[...]).astype(jnp.bfloat16)
    yield
    p = jnp.dot(h, w_ref[...], preferred_element_type=jnp.float32)
    for r in range(ROWS_PER_STEP):
        p_r = p[r * CHUNK:(r + 1) * CHUNK, :]
        z_ref[r] = p_r[:, COL_Z:COL_XBC].astype(z_ref.dtype)
        xbc_ref[r] = p_r[:, COL_XBC:COL_K].astype(xbc_ref.dtype)
        k_ref[r] = p_r[:, COL_K:COL_DT].astype(k_ref.dtype)
        dt_ref[r] = p_r[:, COL_DT:D_PROJ]
    yield
    pt = _bf16_dot_nt(wt_ref[...], h)
    for r in range(ROWS_PER_STEP):
        pt_r = pt[:, r * CHUNK:(r + 1) * CHUNK]
        qt_ref[r] = pt_r[:D_ATTN, :].astype(qt_ref.dtype)
        vt_ref[r] = pt_r[D_ATTN:, :].astype(vt_ref.dtype)


def _split_halves(x, low_half):
    zero = jnp.zeros_like(x)
    return jnp.concatenate([jnp.where(low_half, x, zero), jnp.where(low_half, zero, x)], axis=0)


def _hi_lo(x):
    hi = x.astype(jnp.bfloat16).astype(jnp.float32)
    return hi, x - hi


def _run_interleaved(stage_generators):
    pending = list(stage_generators)
    while pending:
        for gen in list(pending):
            try:
                next(gen)
            except StopIteration:
                pending.remove(gen)


def _stage_bf16(jobs):
    plans = []
    for matrices, stage_ref, sem in jobs:
        slab_rows = stage_ref.shape[1]
        slabs = []
        for src_hbm, dst_ref in matrices:
            assert src_hbm.shape[0] % slab_rows == 0 and src_hbm.shape[1] == stage_ref.shape[2]
            slabs += [(src_hbm, dst_ref, lo) for lo in range(0, src_hbm.shape[0], slab_rows)]
        copies = [pltpu.make_async_copy(src.at[lo:lo + slab_rows, :], stage_ref.at[k % 2],
                                        sem.at[k % 2])
                  for k, (src, _, lo) in enumerate(slabs)]
        plans.append((slabs, copies, stage_ref, slab_rows))
    for _, copies, _, _ in plans:
        copies[0].start()
    for k in range(max(len(copies) for _, copies, _, _ in plans)):
        for slabs, copies, stage_ref, slab_rows in plans:
            if k < len(copies):
                if k + 1 < len(copies):
                    copies[k + 1].start()
                copies[k].wait()
                _, dst_ref, lo = slabs[k]
                dst_ref[lo:lo + slab_rows, :] = stage_ref[k % 2].astype(dst_ref.dtype)


def _layer_kernel(nc, n_groups, sinks_ref, x_ref, vec_ref, w_ref, wt_ref,
                  eye_ref, bias_ref, expand_ref,
                  xres_ref, wo_hbm, wg_hbm, wu_hbm, wd_hbm,
                  o_ref, z_buf, xbc_buf, k_buf, dt_buf, qt_buf, vt_buf, ymix_buf,
                  tail_ref, kext_ref, vtext_ref, state_ref,
                  wo_ref, wg_ref, wu_ref, wd_ref, stage_wide, stage_narrow, sem_wide, sem_narrow):
    assert PIPELINE_LAG == 2 and n_groups > 2 * PIPELINE_LAG
    s = pl.program_id(0)
    c = jnp.maximum(lax.rem(s - 1, nc), 0)
    row = lambda r, lo=0, hi=D_MODEL, n=1: vec_ref.at[r:r + n, lo:hi]
    nmix_ref, nffn_ref, nfin_ref = row(ROW_NORM_MIX), row(ROW_NORM_FFN), row(ROW_NORM_OUT)
    convb_ref, convw_ref = row(ROW_CONV_B), row(ROW_CONV_W, n=CONV_WIDTH)
    nw_ref, dskip_ref = row(ROW_SSM, 0, D_SSM), row(ROW_SSM, D_SSM, 2 * D_SSM)
    dtb_ref, alog_ref = row(ROW_DT, 0, LANES), row(ROW_DT, LANES, 2 * LANES)

    @pl.when(s == 0)
    def _():
        _stage_bf16([([(wg_hbm, wg_ref), (wu_hbm, wu_ref)], stage_wide, sem_wide),
                     ([(wo_hbm, wo_ref), (wd_hbm, wd_ref)], stage_narrow, sem_narrow)])

    @pl.when((c == 0) & (s >= 1))
    def _():
        tail_ref[...] = jnp.zeros_like(tail_ref)
        kext_ref[:, 0:WINDOW, :] = jnp.zeros((ROWS_PER_STEP, WINDOW, D_KV), kext_ref.dtype)
        vtext_ref[:, :, 0:WINDOW] = jnp.zeros((ROWS_PER_STEP, D_KV, WINDOW), vtext_ref.dtype)
        state_ref[...] = jnp.zeros_like(state_ref)

    def step(wr, project=True, mix=True, ffn=True):
        rd = 1 - wr
        stages = []
        for r in range(ROWS_PER_STEP if mix else 0):
            stages.append(_attention_stages(
                c, sinks_ref, k_buf.at[rd, r], qt_buf.at[rd, r], vt_buf.at[rd, r], eye_ref,
                bias_ref, ymix_buf.at[wr, r], kext_ref.at[r], vtext_ref.at[r]))
            stages.append(_ssd_stages(
                z_buf.at[rd, r], xbc_buf.at[rd, r], dt_buf.at[rd, r], expand_ref, convw_ref,
                convb_ref, dtb_ref, alog_ref, dskip_ref, nw_ref, ymix_buf.at[wr, r],
                tail_ref.at[r], state_ref.at[r]))
        if project:
            stages.append(_in_proj_stages(
                x_ref, nmix_ref, w_ref, wt_ref, z_buf.at[wr], xbc_buf.at[wr], k_buf.at[wr],
                dt_buf.at[wr], qt_buf.at[wr], vt_buf.at[wr]))
        if ffn:
            stages.append(_out_ffn_stages(xres_ref, ymix_buf.at[rd], wo_ref, nffn_ref, wg_ref,
                                          wu_ref, wd_ref, nfin_ref, o_ref))
        _run_interleaved(stages)

    last = n_groups + PIPELINE_LAG - 1
    pl.when(s == 0)(functools.partial(step, 0, mix=False, ffn=False))
    pl.when(s == 1)(functools.partial(step, 1, ffn=False))
    pl.when(s == last - 1)(functools.partial(step, (last - 1) % 2, project=False))
    pl.when(s == last)(functools.partial(step, last % 2, project=False, mix=False))
    steady = (s >= PIPELINE_LAG) & (s < n_groups)
    for wr in range(2):
        pl.when(steady & (lax.rem(s, 2) == wr))(functools.partial(step, wr))


def _ssd_stages(z_ref, xbc_ref, dt_ref, expand_ref, convw_ref, convb_ref, dtb_ref,
                alog_ref, dskip_ref, nw_ref, o_ref, tail_ref, state_ref):
    cur = xbc_ref[...].astype(jnp.float32)
    tail = tail_ref[...]
    tail_ref[...] = cur[CHUNK - SUBLANES:, :]
    first_rows = lax.broadcasted_iota(jnp.int32, (SUBLANES, D_CONV), 0)
    delayed = [cur]
    for s in range(1, CONV_WIDTH):
        rolled = pltpu.roll(cur, s, 0)
        head = jnp.where(first_rows < s, pltpu.roll(tail, s, 0), rolled[:SUBLANES, :])
        delayed.append(jnp.concatenate([head, rolled[SUBLANES:, :]], axis=0))
    yield

    dt = jax.nn.softplus(dt_ref[...] + dtb_ref[...])
    a_log2 = -jnp.exp(alog_ref[...]) * LOG2E
    a_cum = dt * a_log2
    row = lax.broadcasted_iota(jnp.int32, (CHUNK, LANES), 0)
    shift = 1
    while shift < CHUNK:
        a_cum = a_cum + jnp.where(row >= shift, pltpu.roll(a_cum, shift, 0), 0.0)
        shift *= 2
    a_cum_t = a_cum.T
    a_last = a_cum[CHUNK - 1:CHUNK, :]
    sub = lax.broadcasted_iota(jnp.int32, (CHUNK, LANES), 1) % DT_REP
    packed = jnp.zeros((CHUNK, LANES), jnp.float32)
    for i, val in enumerate((dt, dt * jnp.exp2(a_last - a_cum), jnp.exp2(a_cum))):
        for j, part in enumerate(_hi_lo(val)):
            packed = jnp.where(sub == 2 * i + j, part, packed)
    spread = _bf16_dot(packed, expand_ref[...])
    dt_x = spread[:, :D_SSM]
    dt_end_x = spread[:, D_SSM:2 * D_SSM]
    off_scale = spread[:, 2 * D_SSM:]
    state_scale = off_scale[CHUNK - 1:CHUNK, :]
    yield

    conv = convb_ref[...]
    for s in range(CONV_WIDTH):
        tap = CONV_WIDTH - 1 - s
        conv = conv + convw_ref[tap:tap + 1, :] * delayed[s]
    xbc = _silu(conv)
    xs = xbc[:, :D_SSM]
    b_all = xbc[:, D_SSM:D_SSM + D_BC]
    c_all = xbc[:, D_SSM + D_BC:]
    yield

    xdt = xs * dt_x
    xdt_end = xs * dt_end_x
    gw = HEADS_PER_GROUP * SSM_HEAD_DIM
    cbs, y_offs = [], []
    for g in range(SSM_GROUPS):
        b_g = b_all[:, g * D_STATE:(g + 1) * D_STATE]
        c_g = c_all[:, g * D_STATE:(g + 1) * D_STATE]
        cbs.append(_bf16_dot_nt(c_g, b_g))
        state = state_ref[g]
        y_offs.append(_bf16_dot(c_g, state) * off_scale[:, g * gw:(g + 1) * gw])
        new_states = _bf16_dot(b_g.T, xdt_end[:, g * gw:(g + 1) * gw])
        state_ref[g] = state * state_scale[:, g * gw:(g + 1) * gw] + new_states
    yield

    t_idx = lax.broadcasted_iota(jnp.int32, (CHUNK, CHUNK), 0)
    s_idx = lax.broadcasted_iota(jnp.int32, (CHUNK, CHUNK), 1)
    causal = t_idx >= s_idx
    low_half = s_idx < SSM_HEAD_DIM
    pairs_per_group = HEADS_PER_GROUP // HEAD_PAIR
    y_tiles = []
    for g in range(SSM_GROUPS):
        for pp in range(pairs_per_group):
            pair = g * pairs_per_group + pp
            weights = []
            for u in range(HEAD_PAIR):
                col = (pair * HEAD_PAIR + u) * DT_REP
                seg = a_cum[:, col:col + 1] - a_cum_t[col:col + 1, :]
                weights.append(cbs[g] * jnp.exp2(jnp.where(causal, seg, -jnp.inf)))
            rhs = _split_halves(xdt[:, pair * LANES:(pair + 1) * LANES], low_half)
            y_diag = _bf16_dot(jnp.concatenate(weights, axis=1), rhs)
            y_tiles.append(y_diag + y_offs[g][:, pp * LANES:(pp + 1) * LANES])
        yield

    y = jnp.concatenate(y_tiles, axis=1) + dskip_ref[...] * xs
    y = y * _silu(z_ref[...].astype(jnp.float32))
    for g in range(SSM_GROUPS):
        yg = y[:, g * gw:(g + 1) * gw]
        o_ref[:, g * gw:(g + 1) * gw] = _rms_norm(
            yg, nw_ref[:, g * gw:(g + 1) * gw]).astype(o_ref.dtype)


def _attention_stages(c, sinks_ref, k_ref, qt_ref, vt_ref, eye_ref, bias_ref, o_ref,
                      kext_ref, vtext_ref):
    kext_ref[WINDOW:2 * WINDOW, :] = k_ref[...]
    vtext_ref[:, WINDOW:2 * WINDOW] = vt_ref[...]
    k_ext = kext_ref[...]
    vt_ext = vtext_ref[...]
    kext_ref[0:WINDOW, :] = k_ref[...]
    vtext_ref[:, 0:WINDOW] = vt_ref[...]
    bias_t = bias_ref[jnp.minimum(c, 1)]
    eye2 = eye_ref[...]
    low_kv = lax.broadcasted_iota(jnp.int32, (2 * WINDOW, LANES), 1) < HEAD_DIM
    k_swap = pltpu.roll(k_ext, HEAD_DIM, 1)
    zero_k = jnp.zeros_like(k_ext)
    zero_vt = jnp.zeros((HEAD_DIM, 2 * WINDOW), vt_ext.dtype)
    pairs_per_kv = Q_PER_KV // HEAD_PAIR
    assert pairs_per_kv == 2
    logits, values_t = [], []
    for j in range(KV_HEADS):
        k_lo = jnp.where(low_kv, k_ext if j == 0 else k_swap, zero_k)
        k_hi = jnp.where(low_kv, zero_k, k_swap if j == 0 else k_ext)
        keys = jnp.concatenate([jnp.concatenate([k_lo, bias_t], axis=1),
                                jnp.concatenate([k_hi, bias_t], axis=1)], axis=0)
        rows = slice(j * pairs_per_kv * LANES, (j + 1) * pairs_per_kv * LANES)
        q_t = qt_ref[rows, :]
        queries = jnp.concatenate(
            [jnp.concatenate([q_t[:LANES, :], q_t[LANES:, :]], axis=1), eye2], axis=0)
        logits.append(jnp.dot(keys, queries, preferred_element_type=jnp.float32))
        v_t = vt_ext[j * HEAD_DIM:(j + 1) * HEAD_DIM, :]
        values_t.append(jnp.concatenate([jnp.concatenate([v_t, zero_vt], axis=1),
                                         jnp.concatenate([zero_vt, v_t], axis=1)], axis=0))
    yield

    top_rows = lax.broadcasted_iota(jnp.int32, (LANES, LANES), 0) < HEAD_DIM
    for j in range(KV_HEADS):
        prob_rows, denom_tiles = [], []
        for u in range(HEAD_PAIR):
            prob_tiles, denoms = [], []
            for pp in range(pairs_per_kv):
                s_h = logits[j][u * 2 * WINDOW:(u + 1) * 2 * WINDOW, pp * LANES:(pp + 1) * LANES]
                sink = sinks_ref[(j * pairs_per_kv + pp) * HEAD_PAIR + u] * LOG2E
                m = jnp.maximum(jnp.max(s_h, axis=0, keepdims=True), sink)
                p = jnp.exp2(s_h - m)
                prob_tiles.append(p.astype(jnp.bfloat16))
                denoms.append(jnp.sum(p, axis=0, keepdims=True) + jnp.exp2(sink - m))
            prob_rows.append(jnp.concatenate(prob_tiles, axis=1))
            denom_tiles.append(denoms)
            yield
        probs_t = jnp.concatenate(prob_rows, axis=0)
        out_t = jnp.dot(values_t[j], probs_t, preferred_element_type=jnp.float32)
        for pp in range(pairs_per_kv):
            denom = jnp.where(top_rows, denom_tiles[0][pp], denom_tiles[1][pp])
            out = (out_t[:, pp * LANES:(pp + 1) * LANES] / denom).T
            lo = D_SSM + (j * pairs_per_kv + pp) * LANES
            o_ref[:, lo:lo + LANES] = out.astype(o_ref.dtype)
        yield


def _mixer_tables():
    eye = np.eye(WINDOW, dtype=np.float32)
    eye2 = np.concatenate([eye, eye], axis=1)
    key = np.arange(2 * WINDOW)[:, None]
    qry = np.arange(WINDOW)[None, :] + WINDOW
    band = (key <= qry) & (key > qry - WINDOW)
    first = band & (key >= WINDOW)
    bias = np.stack([np.where(first, 0.0, MASKED), np.where(band, 0.0, MASKED)])
    lane = np.arange(LANES)[:, None]
    out = np.arange(3 * D_SSM)[None, :]
    quantity = (lane % DT_REP) // 2
    expand = ((out // D_SSM == quantity) & ((out % D_SSM) // SSM_HEAD_DIM == lane // DT_REP)
              & (lane % DT_REP < 6))
    return tuple(jnp.asarray(t.astype(np.float32), dtype=jnp.bfloat16)
                 for t in (eye2, bias, expand))


def _layer_call(x, vec, w_all, w_t, sinks, w_out, w_gate, w_up, w_down):
    bsz, seq, _ = x.shape
    nc = seq // CHUNK
    n_groups = (bsz // ROWS_PER_STEP) * nc
    eye2, bias, expand = _mixer_tables()
    group = lambda width, lag: pl.BlockSpec(
        (ROWS_PER_STEP, CHUNK, width),
        lambda s: (jnp.clip(s - lag, 0, n_groups - 1) // nc, jnp.clip(s - lag, 0, n_groups - 1) % nc, 0))
    resident = lambda shape: pl.BlockSpec(shape, lambda s: (0,) * len(shape),
                                          pipeline_mode=pl.Buffered(1))
    gw = HEADS_PER_GROUP * SSM_HEAD_DIM
    bf16 = jnp.bfloat16
    handoff = lambda rows, cols, dtype: pltpu.VMEM((2, ROWS_PER_STEP, rows, cols), dtype)
    in_hbm = pl.BlockSpec(memory_space=pl.ANY)
    return pl.pallas_call(
        functools.partial(_layer_kernel, nc, n_groups),
        grid=(n_groups + PIPELINE_LAG,),
        in_specs=[
            pl.BlockSpec(memory_space=pltpu.SMEM),
            group(D_MODEL, 0), resident((VEC_ROWS, D_MODEL)), resident((D_MODEL, D_PROJ)),
            resident((D_PROJ_T, D_MODEL)),
            resident(eye2.shape), resident(bias.shape), resident(expand.shape),
            group(D_MODEL, PIPELINE_LAG), in_hbm, in_hbm, in_hbm, in_hbm,
        ],
        out_specs=group(D_MODEL, PIPELINE_LAG),
        out_shape=jax.ShapeDtypeStruct((bsz, seq, D_MODEL), jnp.float32),
        scratch_shapes=[
            handoff(CHUNK, D_SSM, bf16), handoff(CHUNK, D_CONV, bf16), handoff(CHUNK, D_KV, bf16),
            handoff(CHUNK, LANES, jnp.float32), handoff(D_ATTN, CHUNK, bf16),
            handoff(D_KV, CHUNK, bf16),
            handoff(CHUNK, D_MIX, bf16),
            pltpu.VMEM((ROWS_PER_STEP, SUBLANES, D_CONV), jnp.float32),
            pltpu.VMEM((ROWS_PER_STEP, 2 * WINDOW, D_KV), bf16),
            pltpu.VMEM((ROWS_PER_STEP, D_KV, 2 * WINDOW), bf16),
            pltpu.VMEM((ROWS_PER_STEP, SSM_GROUPS, D_STATE, gw), jnp.float32),
            pltpu.VMEM((D_MIX, D_MODEL), bf16), pltpu.VMEM((D_MODEL, D_FF), bf16),
            pltpu.VMEM((D_MODEL, D_FF), bf16), pltpu.VMEM((D_FF, D_MODEL), bf16),
            pltpu.VMEM((2, STAGE_ROWS_WIDE, D_FF), jnp.float32),
            pltpu.VMEM((2, STAGE_ROWS_NARROW, D_MODEL), jnp.float32),
            pltpu.SemaphoreType.DMA((2,)), pltpu.SemaphoreType.DMA((2,)),
        ],
        compiler_params=pltpu.CompilerParams(
            dimension_semantics=("arbitrary",), vmem_limit_bytes=VMEM_LIMIT),
        name="layer",
    )(sinks, x, vec, w_all, w_t, eye2, bias, expand, x, w_out, w_gate, w_up, w_down)


def _out_ffn_stages(x_ref, ymix_ref, wo_ref, nffn_ref, wg_ref, wu_ref, wd_ref, nfin_ref, o_ref):
    rows = ROWS_PER_STEP * CHUNK
    x1 = x_ref[...].reshape(rows, D_MODEL) + jnp.dot(
        ymix_ref[...].reshape(rows, D_MIX), wo_ref[...], preferred_element_type=jnp.float32)
    h = _rms_norm(x1, nffn_ref[...]).astype(jnp.bfloat16)
    yield
    acts = []
    for lo in range(0, D_FF, FF_SLICE):
        hi = min(lo + FF_SLICE, D_FF)
        gate = jnp.dot(h, wg_ref[:, lo:hi], preferred_element_type=jnp.float32)
        up = jnp.dot(h, wu_ref[:, lo:hi], preferred_element_type=jnp.float32)
        acts.append((_silu(gate) * up).astype(jnp.bfloat16))
        yield
    x2 = x1 + jnp.dot(jnp.concatenate(acts, axis=1), wd_ref[...],
                      preferred_element_type=jnp.float32)
    yield
    o_ref[...] = _rms_norm(x2, nfin_ref[...]).reshape(ROWS_PER_STEP, CHUNK, D_MODEL)


def _layer(x, norm_mix_w, w_in, conv_w, conv_b, dt_bias, a_log, d_skip, ssm_norm_w,
           attn_sinks, w_out, norm_ffn_w, w_gate, w_up, w_down, norm_out_w):
    bsz, seq, _ = x.shape
    bf16 = jnp.bfloat16
    s_xbc = D_SSM + D_CONV
    s_dt = s_xbc + SSM_HEADS
    s_q = s_dt + D_ATTN
    s_k = s_q + D_KV
    w_all = jnp.concatenate(
        [w_in[:, :s_xbc], w_in[:, s_q:s_k], jnp.repeat(w_in[:, s_xbc:s_dt], DT_REP, axis=1)],
        axis=1).astype(bf16)
    w_t = jnp.concatenate(
        [w_in[:, s_dt:s_q] * (HEAD_DIM ** -0.5 * LOG2E), w_in[:, s_k:]], axis=1).T.astype(bf16)
    per_head = jnp.concatenate([jnp.repeat(dt_bias, DT_REP), jnp.repeat(a_log, DT_REP)])
    vectors = jnp.stack([norm_mix_w, norm_ffn_w, norm_out_w, conv_b,
                         jnp.concatenate([ssm_norm_w, jnp.repeat(d_skip, SSM_HEAD_DIM)]),
                         jnp.pad(per_head, (0, D_MODEL - 2 * LANES))])
    vec = jnp.concatenate([
        jnp.pad(vectors[:, None, :], ((0, 0), (0, SUBLANES - 1), (0, 0))).reshape(-1, D_MODEL),
        jnp.pad(conv_w, ((0, SUBLANES - CONV_WIDTH), (0, 0)))])
    return _layer_call(x, vec, w_all, w_t, attn_sinks, w_out, w_gate, w_up, w_down)


def kernel(x, norm_mix_w, w_in, conv_w, conv_b, dt_bias, a_log, d_skip, ssm_norm_w,
           attn_sinks, w_out, norm_ffn_w, w_gate, w_up, w_down, norm_final_w):
    depth = norm_mix_w.shape[0]
    assert depth == 1, "final RMSNorm is fused into the single layer's FFN kernel"
    return _layer(x, norm_mix_w[0], w_in[0], conv_w[0], conv_b[0], dt_bias[0], a_log[0],
                  d_skip[0], ssm_norm_w[0], attn_sinks[0], w_out[0], norm_ffn_w[0],
                  w_gate[0], w_up[0], w_down[0], norm_final_w)
```
